```python
import math, functools
import jax, jax.numpy as jnp
from jax import lax
import numpy as np

D_MODEL = 1024
BATCH = 1
SEQ = 16384
DEPTH = 1
DEC_BATCH = 128
DEC_SEQ = 1
PAST_LEN = 8192
PAGE_SIZE = 128

A_HEADS = 8
A_HEAD_DIM = 64
A_WIDTH = A_HEADS * A_HEAD_DIM
A_SCALE = A_HEAD_DIM ** -0.5
MOBA_BLOCK = 256
MOBA_TOPK = 3
Q_CHUNK = 128
ROPE_THETA = 10000.0
PAGES_PER_BLOCK = MOBA_BLOCK // PAGE_SIZE
C_WIDTH = D_MODEL // 2
CONV_WIDTH = 31
M_HEADS = 4
M_HEAD_DIM = 128
M_WIDTH = M_HEADS * M_HEAD_DIM
M_SCALE = M_HEAD_DIM ** -0.5
N_MEM = 256
N_BRANCH = 3
SPLITS = [A_WIDTH, 2 * A_WIDTH, 3 * A_WIDTH, 3 * A_WIDTH + 2 * C_WIDTH, 3 * A_WIDTH + 2 * C_WIDTH + M_WIDTH]
IN_COLS = 3 * A_WIDTH + 2 * C_WIDTH + M_WIDTH + N_BRANCH * D_MODEL
N_EXPERTS = 32
TOP_K = 4
D_FF = D_MODEL
SWIGLU_LIMIT = 7.0
SWIGLU_ALPHA = 1.702
MOE_ROW_BLOCK = 128
EPS = 1e-5
NEG = -1e30

kernel_name = "hybrid_moba_conformer_memory_moe_step"


def rmsnorm(x, g):
    xf = x.astype(jnp.float32)
    y = xf * lax.rsqrt(jnp.mean(xf * xf, axis=-1, keepdims=True) + EPS)
    return (y * g.astype(jnp.float32)).astype(x.dtype)


def layernorm(x, g, b):
    xf = x.astype(jnp.float32)
    mu = jnp.mean(xf, axis=-1, keepdims=True)
    var = jnp.mean(jnp.square(xf - mu), axis=-1, keepdims=True)
    y = (xf - mu) * lax.rsqrt(var + EPS) * g.astype(jnp.float32) + b.astype(jnp.float32)
    return y.astype(x.dtype)


def rope(x, pos):
    half = x.shape[-1] // 2
    inv = ROPE_THETA ** (-jnp.arange(half, dtype=jnp.float32) / half)
    ang = pos.astype(jnp.float32)[:, None] * inv[None, :]
    cos = jnp.cos(ang)[None, :, None, :]
    sin = jnp.sin(ang)[None, :, None, :]
    xf = x.astype(jnp.float32)
    x1, x2 = xf[..., :half], xf[..., half:]
    return jnp.concatenate([x1 * cos - x2 * sin, x2 * cos + x1 * sin], axis=-1).astype(x.dtype)


def moba_select(q, pos, kmean):
    nb = kmean.shape[1]
    j = pos // MOBA_BLOCK
    s = jnp.einsum('bqhd,bnhd->bhqn', q.astype(jnp.float32), kmean.astype(jnp.float32))
    past = jnp.arange(nb)[None, :] < j[:, None]
    s = jnp.where(past[None, None], s, NEG)
    if nb < MOBA_TOPK:
        s = jnp.pad(s, ((0, 0), (0, 0), (0, 0), (0, MOBA_TOPK - nb)), constant_values=NEG)
    _, top = lax.top_k(s, MOBA_TOPK)
    top = jnp.minimum(top, nb - 1).astype(jnp.int32)
    own = jnp.broadcast_to(j.astype(jnp.int32)[None, None, :, None], top.shape[:-1] + (1,))
    idx = jnp.concatenate([top, own], axis=-1)
    valid = jnp.concatenate([jnp.arange(MOBA_TOPK)[None, :] < j[:, None],
                             jnp.ones((pos.shape[0], 1), dtype=bool)], axis=-1)
    return idx, valid


def moba_attend(q, pos, idx, valid, k_sel, v_sel):
    key_pos = idx[..., None] * MOBA_BLOCK + jnp.arange(MOBA_BLOCK)
    mask = valid[None, None, :, :, None] & (key_pos <= pos[None, None, :, None, None])
    logits = jnp.einsum('bqhd,bhqgkd->bhqgk', q, k_sel).astype(jnp.float32) * A_SCALE
    logits = jnp.where(mask, logits, NEG)
    b, h, nq, g, nk = logits.shape
    p = jax.nn.softmax(logits.reshape(b, h, nq, g * nk), axis=-1).reshape(logits.shape).astype(v_sel.dtype)
    return jnp.einsum('bhqgk,bhqgkd->bqhd', p, v_sel)


def moba_prompt(q, k, v):
    B, S, H, D = q.shape
    nb = -(-S // MOBA_BLOCK)
    pad = nb * MOBA_BLOCK - S
    kb = jnp.pad(k, ((0, 0), (0, pad), (0, 0), (0, 0))).reshape(B, nb, MOBA_BLOCK, H, D)
    vb = jnp.pad(v, ((0, 0), (0, pad), (0, 0), (0, 0))).reshape(B, nb, MOBA_BLOCK, H, D)
    kmean = jnp.mean(kb.astype(jnp.float32), axis=2)
    b_i = jnp.arange(B)[:, None, None, None]
    h_i = jnp.arange(H)[None, :, None, None]
    nc = S // Q_CHUNK
    qc = q.reshape(B, nc, Q_CHUNK, H, D).transpose(1, 0, 2, 3, 4)
    pc = jnp.arange(S, dtype=jnp.int32).reshape(nc, Q_CHUNK)

    def one_chunk(args):
        qq, pp = args
        idx, valid = moba_select(qq, pp, kmean)
        k_sel = kb[b_i, idx, :, h_i, :]
        v_sel = vb[b_i, idx, :, h_i, :]
        return moba_attend(qq, pp, idx, valid, k_sel, v_sel)

    out = lax.map(one_chunk, (qc, pc))
    return out.transpose(1, 0, 2, 3, 4).reshape(B, S, H, D)


def moba_sample(cache_k, cache_v, page_table, q, k_new, v_new):
    DB, T, H, D = q.shape
    n_pages = page_table.shape[1]
    past = n_pages * PAGE_SIZE
    nb = -(-(past + T) // MOBA_BLOCK)
    page_sums = jnp.sum(cache_k[page_table].astype(jnp.float32), axis=2)
    page_blk = jnp.arange(n_pages) // PAGES_PER_BLOCK
    cached_sums = jax.ops.segment_sum(page_sums.transpose(1, 0, 2, 3), page_blk, num_segments=nb)
    new_blk = (past + jnp.arange(T)) // MOBA_BLOCK
    new_sums = jax.ops.segment_sum(k_new.astype(jnp.float32).transpose(1, 0, 2, 3), new_blk, num_segments=nb)
    kmean = ((cached_sums + new_sums) / MOBA_BLOCK).transpose(1, 0, 2, 3)
    pos = past + jnp.arange(T, dtype=jnp.int32)
    idx, valid = moba_select(q, pos, kmean)
    rows = idx[..., None] * MOBA_BLOCK + jnp.arange(MOBA_BLOCK)
    in_cache = rows < past
    b_i = jnp.arange(DB)[:, None, None, None, None]
    h_i = jnp.arange(H)[None, :, None, None, None]
    phys = page_table[b_i, jnp.minimum(rows // PAGE_SIZE, n_pages - 1)]
    slot = rows % PAGE_SIZE
    new_row = jnp.clip(rows - past, 0, T - 1)

    def fetch(cache, new):
        from_cache = cache[phys, slot, h_i]
        from_new = new[b_i, new_row, h_i]
        return jnp.where(in_cache[..., None], from_cache, from_new)

    return moba_attend(q, pos, idx, valid, fetch(cache_k, k_new), fetch(cache_v, v_new))


def conformer_conv(u, conv_state, w_dw, b_dw, conv_ln_g, conv_ln_b):
    a, gate = jnp.split(u, 2, axis=-1)
    h = a * jax.nn.sigmoid(gate)
    hp = jnp.concatenate([conv_state.astype(h.dtype), h], axis=1)
    y = lax.conv_general_dilated(hp, w_dw[:, None, :].astype(h.dtype), window_strides=(1,), padding='VALID',
                                 dimension_numbers=('NWC', 'WIO', 'NWC'), feature_group_count=C_WIDTH) + b_dw
    y = jax.nn.silu(layernorm(y, conv_ln_g, conv_ln_b))
    return y, hp[:, -(CONV_WIDTH - 1):]


def memory_kv(mem, norm_mem_g, w_mem_kv):
    B = mem.shape[0]
    kv = rmsnorm(mem, norm_mem_g) @ w_mem_kv
    mk, mv = jnp.split(kv, 2, axis=-1)
    return mk.reshape(B, N_MEM, M_HEADS, M_HEAD_DIM), mv.reshape(B, N_MEM, M_HEADS, M_HEAD_DIM)


def memory_attend(q, mk, mv):
    s = jnp.einsum('bthd,bmhd->bhtm', q, mk).astype(jnp.float32) * M_SCALE
    p = jax.nn.softmax(s, axis=-1).astype(mv.dtype)
    return jnp.einsum('bhtm,bmhd->bthd', p, mv)


def moe_ffn(x, w_router, b_router, w_gate_up, b_gate_up, w_down, b_down):
    B, T, D = x.shape
    xt = x.reshape(-1, D)
    n_tok = xt.shape[0]
    logits = (xt @ w_router + b_router).astype(jnp.float32)
    top_val, top_e = lax.top_k(logits, TOP_K)
    gate_w = jax.nn.softmax(top_val, axis=-1).astype(x.dtype)
    n_asg = n_tok * TOP_K
    flat_e = top_e.reshape(-1).astype(jnp.int32)
    flat_tok = jnp.repeat(jnp.arange(n_tok, dtype=jnp.int32), TOP_K)
    flat_w = gate_w.reshape(-1)
    order = jnp.argsort(flat_e)
    s_e, s_tok, s_w = flat_e[order], flat_tok[order], flat_w[order]
    counts = jnp.bincount(flat_e, length=N_EXPERTS)
    padded = (counts + MOE_ROW_BLOCK - 1) // MOE_ROW_BLOCK * MOE_ROW_BLOCK
    start = jnp.cumsum(counts) - counts
    pad_end = jnp.cumsum(padded)
    pad_start = pad_end - padded
    dest = pad_start[s_e] + (jnp.arange(n_asg) - start[s_e])
    n_blocks = -(-(n_asg + N_EXPERTS * (MOE_ROW_BLOCK - 1)) // MOE_ROW_BLOCK)
    n_rows = n_blocks * MOE_ROW_BLOCK
    row_tok = jnp.zeros((n_rows,), jnp.int32).at[dest].set(s_tok)
    blk_e = jnp.minimum(jnp.searchsorted(pad_end, jnp.arange(n_blocks) * MOE_ROW_BLOCK, side='right'), N_EXPERTS - 1)

    def expert_block(args):
        e, toks = args
        gu = xt[toks] @ w_gate_up[e] + b_gate_up[e]
        g, u = jnp.split(gu, 2, axis=-1)
        g = jnp.minimum(g, SWIGLU_LIMIT)
        u = jnp.clip(u, -SWIGLU_LIMIT, SWIGLU_LIMIT)
        hdn = g * jax.nn.sigmoid(SWIGLU_ALPHA * g) * (u + 1.0)
        return hdn @ w_down[e] + b_down[e]

    out_rows = lax.map(expert_block, (blk_e, row_tok.reshape(n_blocks, MOE_ROW_BLOCK))).reshape(n_rows, D)
    y = jnp.zeros_like(xt).at[s_tok].add(out_rows[dest] * s_w[:, None])
    return y.reshape(B, T, D)


def hybrid_layer(x, pos, conv_state, mem_k, mem_v, moba_fn,
                 norm_mix_g, w_in, w_dw, b_dw, conv_ln_g, conv_ln_b,
                 w_branch_a, w_branch_c, w_branch_m, w_out, norm_ffn_g,
                 w_router, b_router, w_gate_up, b_gate_up, w_down, b_down):
    B, T, _ = x.shape
    xn = rmsnorm(x, norm_mix_g)
    q_a, k_a, v_a, u_c, q_m, g_all = jnp.split(xn @ w_in, SPLITS, axis=-1)
    q_a = rope(q_a.reshape(B, T, A_HEADS, A_HEAD_DIM), pos)
    k_a = rope(k_a.reshape(B, T, A_HEADS, A_HEAD_DIM), pos)
    v_a = v_a.reshape(B, T, A_HEADS, A_HEAD_DIM)
    o_a = moba_fn(q_a, k_a, v_a).reshape(B, T, A_WIDTH)
    o_c, new_conv = conformer_conv(u_c, conv_state, w_dw, b_dw, conv_ln_g, conv_ln_b)
    o_m = memory_attend(q_m.reshape(B, T, M_HEADS, M_HEAD_DIM), mem_k, mem_v).reshape(B, T, M_WIDTH)
    gates = jax.nn.sigmoid(g_all.reshape(B, T, N_BRANCH, D_MODEL))
    merged = (gates[:, :, 0] * (o_a @ w_branch_a) + gates[:, :, 1] * (o_c @ w_branch_c)
              + gates[:, :, 2] * (o_m @ w_branch_m))
    h = x + merged @ w_out
    h = h + moe_ffn(rmsnorm(h, norm_ffn_g), w_router, b_router, w_gate_up, b_gate_up, w_down, b_down)
    return h, k_a, v_a, new_conv


def setup_inputs(seed: int = 0) -> dict:
    key = jax.random.key(seed)
    ks = jax.random.split(key, 40)
    f32 = jnp.float32
    n_pages = PAST_LEN // PAGE_SIZE
    n_used = DEC_BATCH * n_pages
    n_phys = n_used + n_used // 4
    nrm = lambda k, shape, s: jax.random.normal(k, shape, f32) * s
    page_table = jax.random.permutation(ks[0], n_phys)[:n_used].reshape(DEC_BATCH, n_pages).astype(jnp.int32)
    return {
        "x_prompt": nrm(ks[1], (BATCH, SEQ, D_MODEL), 1.0),
        "x_sample": nrm(ks[2], (DEC_BATCH, DEC_SEQ, D_MODEL), 1.0),
        "mem_prompt": nrm(ks[3], (BATCH, N_MEM, D_MODEL), 1.0),
        "cache_k": nrm(ks[4], (n_phys, PAGE_SIZE, A_HEADS, A_HEAD_DIM), 1.0),
        "cache_v": nrm(ks[5], (n_phys, PAGE_SIZE, A_HEADS, A_HEAD_DIM), 1.0),
        "page_table": page_table,
        "state_conv": nrm(ks[6], (DEC_BATCH, CONV_WIDTH - 1, C_WIDTH), 0.5),
        "cache_mem_k": nrm(ks[7], (DEC_BATCH, N_MEM, M_HEADS, M_HEAD_DIM), 1.0),
        "cache_mem_v": nrm(ks[8], (DEC_BATCH, N_MEM, M_HEADS, M_HEAD_DIM), 1.0),
        "norm_mix_g": 1.0 + nrm(ks[9], (D_MODEL,), 0.02),
        "w_in": nrm(ks[10], (D_MODEL, IN_COLS), D_MODEL ** -0.5),
        "w_dw": nrm(ks[11], (CONV_WIDTH, C_WIDTH), CONV_WIDTH ** -0.5),
        "b_dw": nrm(ks[12], (C_WIDTH,), 0.01),
        "conv_ln_g": 1.0 + nrm(ks[13], (C_WIDTH,), 0.02),
        "conv_ln_b": nrm(ks[14], (C_WIDTH,), 0.01),
        "norm_mem_g": 1.0 + nrm(ks[15], (D_MODEL,), 0.02),
        "w_mem_kv": nrm(ks[16], (D_MODEL, 2 * M_WIDTH), D_MODEL ** -0.5),
        "w_branch_a": nrm(ks[17], (A_WIDTH, D_MODEL), A_WIDTH ** -0.5),
        "w_branch_c": nrm(ks[18], (C_WIDTH, D_MODEL), C_WIDTH ** -0.5),
        "w_branch_m": nrm(ks[19], (M_WIDTH, D_MODEL), M_WIDTH ** -0.5),
        "w_out": nrm(ks[20], (D_MODEL, D_MODEL), D_MODEL ** -0.5),
        "norm_ffn_g": 1.0 + nrm(ks[21], (D_MODEL,), 0.02),
        "w_router": nrm(ks[22], (D_MODEL, N_EXPERTS), D_MODEL ** -0.5),
        "b_router": nrm(ks[23], (N_EXPERTS,), 0.01),
        "w_gate_up": nrm(ks[24], (N_EXPERTS, D_MODEL, 2 * D_FF), D_MODEL ** -0.5),
        "b_gate_up": nrm(ks[25], (N_EXPERTS, 2 * D_FF), 0.01),
        "w_down": nrm(ks[26], (N_EXPERTS, D_FF, D_MODEL), D_FF ** -0.5),
        "b_down": nrm(ks[27], (N_EXPERTS, D_MODEL), 0.01),
        "norm_final_g": 1.0 + nrm(ks[28], (D_MODEL,), 0.02),
    }


def reference(x_prompt, x_sample, mem_prompt, cache_k, cache_v, page_table, state_conv, cache_mem_k, cache_mem_v,
              norm_mix_g, w_in, w_dw, b_dw, conv_ln_g, conv_ln_b, norm_mem_g, w_mem_kv,
              w_branch_a, w_branch_c, w_branch_m, w_out, norm_ffn_g, w_router, b_router,
              w_gate_up, b_gate_up, w_down, b_down, norm_final_g):
    pos_p = jnp.arange(x_prompt.shape[1], dtype=jnp.int32)
    pos_s = page_table.shape[1] * PAGE_SIZE + jnp.arange(x_sample.shape[1], dtype=jnp.int32)
    conv0 = jnp.zeros((x_prompt.shape[0], CONV_WIDTH - 1, C_WIDTH), x_prompt.dtype)
    sample_moba = functools.partial(moba_sample, cache_k, cache_v, page_table)
    h_p, h_s = x_prompt, x_sample
    for _ in range(DEPTH):
        mk_p, mv_p = memory_kv(mem_prompt, norm_mem_g, w_mem_kv)
        h_p, k_p, v_p, conv_p = hybrid_layer(
            h_p, pos_p, conv0, mk_p, mv_p, moba_prompt,
            norm_mix_g, w_in, w_dw, b_dw, conv_ln_g, conv_ln_b, w_branch_a, w_branch_c, w_branch_m, w_out,
            norm_ffn_g, w_router, b_router, w_gate_up, b_gate_up, w_down, b_down)
        h_s, k_s, v_s, conv_s = hybrid_layer(
            h_s, pos_s, state_conv, cache_mem_k, cache_mem_v, sample_moba,
            norm_mix_g, w_in, w_dw, b_dw, conv_ln_g, conv_ln_b, w_branch_a, w_branch_c, w_branch_m, w_out,
            norm_ffn_g, w_router, b_router, w_gate_up, b_gate_up, w_down, b_down)
    y_prompt = rmsnorm(h_p, norm_final_g)
    y_sample = rmsnorm(h_s, norm_final_g)
    return (y_prompt, y_sample, k_p, v_p, conv_p, mk_p, mv_p, k_s, v_s, conv_s)
```

```python
import functools

import numpy as np
import jax
import jax.numpy as jnp
from jax import lax
from jax.experimental import pallas as pl
from jax.experimental.pallas import tpu as pltpu

F32 = jnp.float32
BF16 = jnp.bfloat16
I32 = jnp.int32

A_HEADS = 8
A_HEAD_DIM = 64
A_WIDTH = A_HEADS * A_HEAD_DIM
A_SCALE = A_HEAD_DIM ** -0.5
MOBA_BLOCK = 256
MOBA_TOPK = 3
ROPE_THETA = 10000.0
N_BRANCH = 3
TOP_K = 4
SWIGLU_LIMIT = 7.0
SWIGLU_ALPHA = 1.702
EPS = 1e-5
NEG = -1e30

LANES = 128
VMEM_LIMIT = 56 * 1024 * 1024
MOE_ROW_TILE = 256


def _cparams(sem):
    return pltpu.CompilerParams(dimension_semantics=sem, vmem_limit_bytes=VMEM_LIMIT)


def _row_tile(n, cap):
    t = min(cap, n) // 8 * 8
    while n % t:
        t -= 8
    return t


def _const_spec(shape):
    nd = len(shape)
    return pl.BlockSpec(shape, lambda *_: (0,) * nd, pipeline_mode=pl.Buffered(1))


def _rms(x, g):
    return x * lax.rsqrt(jnp.mean(x * x, axis=-1, keepdims=True) + EPS) * g


def _split_bf16(x):
    hi = x.astype(BF16)
    lo = (x - hi.astype(F32)).astype(BF16)
    return hi, lo


def _dot(a, b):
    return jnp.dot(a, b, preferred_element_type=F32)


def _dot_t(a, b):
    return lax.dot_general(a, b, (((1,), (1,)), ((), ())), preferred_element_type=F32)


def _memkv_kernel(mem_ref, g_ref, w_ref, kv_ref):
    xn = _rms(mem_ref[...], g_ref[...])
    kv_ref[...] = _dot(xn.astype(BF16), w_ref[...].astype(BF16))


def _memory_kv(mem2d, g, w):
    n, d = mem2d.shape
    return pl.pallas_call(
        _memkv_kernel,
        out_shape=jax.ShapeDtypeStruct((n, w.shape[1]), F32),
        compiler_params=_cparams(None),
        name="memory_kv",
    )(mem2d, g.reshape(1, d), w)


def _inproj_kernel(x_ref, g_ref, w_ref, cos_ref, sin_ref, q_ref, kf_ref, kb_ref, vf_ref, vb_ref,
                   h_ref, qm_ref, gt_ref, *maybe_ks_ref, m_scale):
    tm = x_ref.shape[0]
    aw = q_ref.shape[1]
    cw = h_ref.shape[1]
    mw = qm_ref.shape[1]
    xb = _rms(x_ref[...], g_ref[...]).astype(BF16)

    def proj(c0, n):
        return _dot(xb, w_ref[:, c0:c0 + n])

    cos = cos_ref[...]
    sin = sin_ref[...]
    lane = lax.broadcasted_iota(I32, (tm, LANES), 1)
    first_half = (lane % A_HEAD_DIM) < (A_HEAD_DIM // 2)

    def rope(c):
        swapped = jnp.where(first_half, pltpu.roll(c, LANES - A_HEAD_DIM // 2, 1),
                            pltpu.roll(c, A_HEAD_DIM // 2, 1))
        return c * cos + swapped * sin

    q = proj(0, aw)
    k = proj(aw, aw)
    for j in range(aw // LANES):
        sl = slice(j * LANES, (j + 1) * LANES)
        q_ref[:, sl] = (rope(q[:, sl]) * A_SCALE).astype(BF16)
        kc = rope(k[:, sl])
        kf_ref[:, sl] = kc
        kb_ref[:, sl] = kc.astype(BF16)
    if maybe_ks_ref:
        ks_ref, = maybe_ks_ref
        for b in range(tm // MOBA_BLOCK):
            ks_ref[b] = jnp.sum(kf_ref[b * MOBA_BLOCK:(b + 1) * MOBA_BLOCK, :], axis=0, keepdims=True)
    v = proj(2 * aw, aw)
    vf_ref[...] = v
    vb_ref[...] = v.astype(BF16)
    a = proj(3 * aw, cw)
    gate = proj(3 * aw + cw, cw)
    h_ref[...] = a * jax.nn.sigmoid(gate)
    c0 = 3 * aw + 2 * cw
    qm_ref[...] = (proj(c0, mw) * m_scale).astype(BF16)
    c0 += mw
    gw = gt_ref.shape[1]
    step = 512
    for j in range(gw // step):
        gt_ref[:, j * step:(j + 1) * step] = jax.nn.sigmoid(proj(c0 + j * step, step))


def _in_projection(x2d, g, w_bf, cos, sin, *, tm, cw, mw, m_scale, block_sums):
    r, d = x2d.shape
    aw = A_WIDTH
    gw = w_bf.shape[1] - (3 * aw + 2 * cw + mw)
    row = lambda width: pl.BlockSpec((tm, width), lambda i: (i, 0))
    out_shape = [
        jax.ShapeDtypeStruct((r, aw), BF16), jax.ShapeDtypeStruct((r, aw), F32), jax.ShapeDtypeStruct((r, aw), BF16),
        jax.ShapeDtypeStruct((r, aw), F32), jax.ShapeDtypeStruct((r, aw), BF16),
        jax.ShapeDtypeStruct((r, cw), F32), jax.ShapeDtypeStruct((r, mw), BF16), jax.ShapeDtypeStruct((r, gw), F32)]
    out_specs = [row(aw), row(aw), row(aw), row(aw), row(aw), row(cw), row(mw), row(gw)]
    if block_sums:
        assert tm % MOBA_BLOCK == 0
        bpt = tm // MOBA_BLOCK
        out_shape.append(jax.ShapeDtypeStruct((r // MOBA_BLOCK, 1, aw), F32))
        out_specs.append(pl.BlockSpec((bpt, 1, aw), lambda i: (i, 0, 0)))
    return pl.pallas_call(
        functools.partial(_inproj_kernel, m_scale=m_scale),
        grid=(r // tm,),
        in_specs=[row(d), _const_spec((1, d)), _const_spec(w_bf.shape), row(LANES), row(LANES)],
        out_specs=out_specs,
        out_shape=out_shape,
        compiler_params=_cparams(("arbitrary",)),
        name="in_projection",
    )(x2d, g.reshape(1, d), w_bf, cos, sin)


def _rope_tables(pos):
    half = A_HEAD_DIM // 2
    inv = ROPE_THETA ** (-jnp.arange(half, dtype=F32) / half)
    ang = pos.astype(F32)[:, None] * inv[None, :]
    cos = jnp.cos(ang)
    sin = jnp.sin(ang)
    reps = LANES // A_HEAD_DIM
    return (jnp.tile(jnp.concatenate([cos, cos], axis=-1), (1, reps)),
            jnp.tile(jnp.concatenate([-sin, sin], axis=-1), (1, reps)))


def _moba_kernel(qi_ref, ki_ref, q_ref, k_ref, v_ref, ks_ref, o_ref, qaug, m_scr, l_scr, acc, *, tq):
    step = pl.program_id(0)
    qi = qi_ref[step]
    kb = ki_ref[step]
    dh = A_HEAD_DIM
    n_chunks = A_WIDTH // LANES
    lane_q = lax.broadcasted_iota(I32, (tq, LANES), 1)
    row_q = lax.broadcasted_iota(I32, (tq, LANES), 0)
    row_blk = (qi * tq + row_q) // MOBA_BLOCK
    low_half = lane_q < dh

    @pl.when(kb == 0)
    def _select():
        m_scr[...] = jnp.full(m_scr.shape, -jnp.inf, F32)
        l_scr[...] = jnp.zeros(l_scr.shape, F32)
        acc[...] = jnp.zeros(acc.shape, F32)
        lane_k = lax.broadcasted_iota(I32, (dh, LANES), 1)
        zeros = jnp.zeros((dh, LANES), F32)
        for c in range(n_chunks):
            qc = q_ref[:, c * LANES:(c + 1) * LANES]
            ksc = ks_ref[:, c * LANES:(c + 1) * LANES]
            for par in range(2):
                h = 2 * c + par
                mine = (lane_k < dh) if par == 0 else (lane_k >= dh)
                km = jnp.where(mine, ksc, 0.0)
                km = jnp.concatenate([zeros, km] if par == 0 else [km, zeros], axis=0)
                hi, lo = _split_bf16(km)
                sc = _dot_t(qc, hi) + _dot_t(qc, lo)
                col = lane_q - dh if par == 0 else lane_q
                in_seg = (col >= 0) & (col < dh)
                s = jnp.where(in_seg & (col < row_blk), sc, NEG)
                s = jnp.where(in_seg, s, -jnp.inf)
                sel = in_seg & (col == row_blk)
                for t in range(MOBA_TOPK):
                    mx = jnp.max(s, axis=-1, keepdims=True)
                    idx = jnp.min(jnp.where(s == mx, col, 2 * LANES), axis=-1, keepdims=True)
                    pick = col == idx
                    sel = sel | (pick & (t < row_blk))
                    s = jnp.where(pick, -jnp.inf, s)
                bias = jnp.where(in_seg & jnp.logical_not(sel), NEG, 0.0).astype(BF16)
                keep = jnp.where(in_seg, 0.0, 1.0).astype(BF16)
                qaug[h] = qc * keep + bias

    def sweep(masked):
        lane_k = lax.broadcasted_iota(I32, (MOBA_BLOCK, LANES), 1)
        keep = [jnp.where(lane_k < dh, 1.0, 0.0).astype(BF16), jnp.where(lane_k >= dh, 1.0, 0.0).astype(BF16)]
        onehot = [jnp.where(lane_k - dh == kb, 1.0, 0.0).astype(BF16), jnp.where(lane_k == kb, 1.0, 0.0).astype(BF16)]
        if masked:
            qpos = qi * tq + lax.broadcasted_iota(I32, (tq, MOBA_BLOCK), 0)
            kpos = kb * MOBA_BLOCK + lax.broadcasted_iota(I32, (tq, MOBA_BLOCK), 1)
            causal = kpos <= qpos
        for c in range(n_chunks):
            kc = k_ref[:, c * LANES:(c + 1) * LANES]
            vc = v_ref[:, c * LANES:(c + 1) * LANES]
            upd = []
            for par in range(2):
                h = 2 * c + par
                ka = kc * keep[par] + onehot[par]
                s = _dot_t(qaug[h], ka)
                if masked:
                    s = jnp.where(causal, s, NEG)
                m_prev = m_scr[h]
                m_new = jnp.maximum(m_prev, jnp.max(s, axis=-1, keepdims=True))
                alpha = jnp.exp(m_prev - m_new)
                p = jnp.exp(s - m_new)
                l_scr[h] = alpha * l_scr[h] + jnp.sum(p, axis=-1, keepdims=True)
                m_scr[h] = m_new
                upd.append((alpha, _dot(p.astype(BF16), vc)))
            a_c = acc[c]
            acc[c] = jnp.where(low_half, upd[0][0] * a_c + upd[0][1], upd[1][0] * a_c + upd[1][1])

    needs_mask = (kb + 1) * MOBA_BLOCK > qi * tq
    pl.when(needs_mask)(lambda: sweep(True))
    pl.when(jnp.logical_not(needs_mask))(lambda: sweep(False))

    @pl.when(kb == (qi * tq + tq - 1) // MOBA_BLOCK)
    def _finish():
        for c in range(n_chunks):
            inv = jnp.where(low_half, 1.0 / l_scr[2 * c], 1.0 / l_scr[2 * c + 1])
            o_ref[:, c * LANES:(c + 1) * LANES] = (acc[c] * inv).astype(BF16)


def _moba_prompt(q, kb, vb, ksum, *, tq):
    r, aw = q.shape
    nb = r // MOBA_BLOCK
    assert r % MOBA_BLOCK == 0 and r % tq == 0 and nb <= A_HEAD_DIM and MOBA_TOPK <= nb
    ks = jnp.zeros((A_HEAD_DIM, aw), F32).at[:nb].set(ksum)
    qi, ki = [], []
    for i in range(r // tq):
        for n in range((i * tq + tq - 1) // MOBA_BLOCK + 1):
            qi.append(i)
            ki.append(n)
    grid_spec = pltpu.PrefetchScalarGridSpec(
        num_scalar_prefetch=2,
        grid=(len(qi),),
        in_specs=[pl.BlockSpec((tq, aw), lambda s, qi, ki: (qi[s], 0)),
                  pl.BlockSpec((MOBA_BLOCK, aw), lambda s, qi, ki: (ki[s], 0)),
                  pl.BlockSpec((MOBA_BLOCK, aw), lambda s, qi, ki: (ki[s], 0)),
                  pl.BlockSpec(ks.shape, lambda s, qi, ki: (0, 0))],
        out_specs=pl.BlockSpec((tq, aw), lambda s, qi, ki: (qi[s], 0)),
        scratch_shapes=[pltpu.VMEM((A_HEADS, tq, LANES), BF16),
                        pltpu.VMEM((A_HEADS, tq, 1), F32),
                        pltpu.VMEM((A_HEADS, tq, 1), F32),
                        pltpu.VMEM((aw // LANES, tq, LANES), F32)])
    return pl.pallas_call(
        functools.partial(_moba_kernel, tq=tq),
        grid_spec=grid_spec,
        out_shape=jax.ShapeDtypeStruct((r, aw), BF16),
        compiler_params=_cparams(("arbitrary",)),
        name="moba_prompt",
    )(jnp.asarray(np.array(qi, np.int32)), jnp.asarray(np.array(ki, np.int32)), q, kb, vb, ks)


def _ln_silu(y, g, b):
    mu = jnp.mean(y, axis=-1, keepdims=True)
    d = y - mu
    var = jnp.mean(d * d, axis=-1, keepdims=True)
    z = d * lax.rsqrt(var + EPS) * g + b
    return z * jax.nn.sigmoid(z)


def _conv_prompt_kernel(h_ref, st_ref, w_ref, b_ref, g_ref, beta_ref, o_ref, ext, *, halo, rc):
    tm = h_ref.shape[0]
    kw = w_ref.shape[0]

    @pl.when(pl.program_id(0) == 0)
    def _():
        ext[0:halo, :] = st_ref[...]

    ext[halo:halo + tm, :] = h_ref[...]
    off = halo - (kw - 1)
    for r0 in range(0, tm, rc):
        y = jnp.zeros((rc, h_ref.shape[1]), F32) + b_ref[...]
        for k in range(kw):
            y = y + w_ref[k:k + 1, :] * ext[r0 + off + k:r0 + off + k + rc, :]
        o_ref[r0:r0 + rc, :] = _ln_silu(y, g_ref[...], beta_ref[...]).astype(BF16)
    ext[0:halo, :] = ext[tm:tm + halo, :]


def _conv_prompt(h2d, state, w_dw, b_dw, ln_g, ln_b, *, tm):
    r, c = h2d.shape
    kw = w_dw.shape[0]
    halo = -(-(kw - 1) // 8) * 8
    st = jnp.zeros((halo, c), F32).at[halo - (kw - 1):].set(state)
    vec = lambda a: a.reshape(1, c)
    return pl.pallas_call(
        functools.partial(_conv_prompt_kernel, halo=halo, rc=64),
        grid=(r // tm,),
        in_specs=[pl.BlockSpec((tm, c), lambda i: (i, 0)), _const_spec((halo, c)), _const_spec((kw, c)),
                  _const_spec((1, c)), _const_spec((1, c)), _const_spec((1, c))],
        out_specs=pl.BlockSpec((tm, c), lambda i: (i, 0)),
        out_shape=jax.ShapeDtypeStruct((r, c), BF16),
        scratch_shapes=[pltpu.VMEM((tm + halo, c), F32)],
        compiler_params=_cparams(("arbitrary",)),
        name="conv_prompt",
    )(h2d, st, w_dw, vec(b_dw), vec(ln_g), vec(ln_b))


def _conv_step_kernel(st_ref, h_ref, w_ref, b_ref, g_ref, beta_ref, o_ref):
    kw = w_ref.shape[0]
    y = b_ref[...] + w_ref[kw - 1:kw, :] * h_ref[...]
    for k in range(kw - 1):
        y = y + w_ref[k:k + 1, :] * st_ref[k]
    o_ref[...] = _ln_silu(y, g_ref[...], beta_ref[...]).astype(BF16)


def _conv_step(state_t, h2d, w_dw, b_dw, ln_g, ln_b):
    b, c = h2d.shape
    vec = lambda a: a.reshape(1, c)
    return pl.pallas_call(
        _conv_step_kernel,
        out_shape=jax.ShapeDtypeStruct((b, c), BF16),
        compiler_params=_cparams(None),
        name="conv_step",
    )(state_t, h2d, w_dw, vec(b_dw), vec(ln_g), vec(ln_b))


def _memattn_kernel(q_ref, k_ref, v_ref, o_ref, *, hd):
    for h in range(q_ref.shape[1] // hd):
        sl = slice(h * hd, (h + 1) * hd)
        s = _dot_t(q_ref[:, sl], k_ref[:, sl])
        p = jnp.exp(s - jnp.max(s, axis=-1, keepdims=True))
        inv = 1.0 / jnp.sum(p, axis=-1, keepdims=True)
        o_ref[:, sl] = (_dot(p.astype(BF16), v_ref[:, sl]) * inv).astype(BF16)


def _memattn_prompt(qm, mk_bf, mv_bf, *, tm, hd):
    r, w = qm.shape
    return pl.pallas_call(
        functools.partial(_memattn_kernel, hd=hd),
        grid=(r // tm,),
        in_specs=[pl.BlockSpec((tm, w), lambda i: (i, 0)), _const_spec(mk_bf.shape), _const_spec(mv_bf.shape)],
        out_specs=pl.BlockSpec((tm, w), lambda i: (i, 0)),
        out_shape=jax.ShapeDtypeStruct((r, w), BF16),
        compiler_params=_cparams(("arbitrary",)),
        name="memattn_prompt",
    )(qm, mk_bf, mv_bf)


def _decode_attend_kernel(q_ref, k_ref, v_ref, *rest, hd, has_new):
    if has_new:
        kn_ref, vn_ref, o_ref = rest
    else:
        o_ref, = rest
    w = q_ref.shape[2]
    head_of_lane = lax.broadcasted_iota(I32, (LANES, w), 1) // hd
    head_row = lax.broadcasted_iota(I32, (LANES, w), 0)
    own = head_of_lane == head_row
    expand = jnp.where(own, 1.0, 0.0).astype(BF16)
    q = q_ref[0].astype(F32)
    q_heads = jnp.where(own, q, 0.0).astype(BF16)
    s = _dot_t(k_ref[0].astype(BF16), q_heads)
    m = jnp.max(s, axis=0, keepdims=True)
    if has_new:
        hi, lo = _split_bf16(jnp.broadcast_to(q * kn_ref[0], (8, w)))
        s_new = (_dot_t(hi, expand) + _dot_t(lo, expand))
        m = jnp.maximum(m, s_new[0:1])
    pe = _dot(jnp.exp(s - m).astype(BF16), expand)
    num = jnp.sum(pe * v_ref[0], axis=0, keepdims=True)
    den = jnp.sum(pe, axis=0, keepdims=True)
    if has_new:
        pne = _dot(jnp.exp(s_new - m).astype(BF16), expand)[0:1]
        num = num + pne * vn_ref[0]
        den = den + pne
    o_ref[0] = (num / den).astype(BF16)


def _decode_attend(q, k, v, k_new=None, v_new=None, *, hd):
    b, w = q.shape
    nk = k.shape[1]
    assert w // hd <= LANES
    has_new = k_new is not None
    one = pl.BlockSpec((1, 1, w), lambda i: (i, 0, 0))
    many = pl.BlockSpec((1, nk, w), lambda i: (i, 0, 0))
    args = [q.reshape(b, 1, w), k, v]
    in_specs = [one, many, many]
    if has_new:
        args += [k_new.reshape(b, 1, w), v_new.reshape(b, 1, w)]
        in_specs += [one, one]
    out = pl.pallas_call(
        functools.partial(_decode_attend_kernel, hd=hd, has_new=has_new),
        grid=(b,),
        in_specs=in_specs,
        out_specs=one,
        out_shape=jax.ShapeDtypeStruct((b, 1, w), BF16),
        compiler_params=_cparams(("arbitrary",)),
        name="decode_attend",
    )(*args)
    return out.reshape(b, w)


def _pagesum_kernel(pt_ref, *refs, pages_per_block):
    del pt_ref
    o_ref = refs[-1]
    page_refs = refs[:-1]
    for m in range(len(page_refs) // pages_per_block):
        tot = None
        for j in range(pages_per_block):
            s = jnp.sum(page_refs[m * pages_per_block + j][0], axis=0)
            tot = s if tot is None else tot + s
        o_ref[0, m] = tot


def _page_block_sums(cache_k, page_table, *, pages_per_step):
    db, n_pages = page_table.shape
    _, page, nh, dh = cache_k.shape
    ppb = MOBA_BLOCK // page
    assert n_pages % pages_per_step == 0 and pages_per_step % ppb == 0
    bps = pages_per_step // ppb

    def page_spec(j):
        return pl.BlockSpec((1, page, nh, dh), lambda b, s, pt: (pt[b, s * pages_per_step + j], 0, 0, 0))

    grid_spec = pltpu.PrefetchScalarGridSpec(
        num_scalar_prefetch=1,
        grid=(db, n_pages // pages_per_step),
        in_specs=[page_spec(j) for j in range(pages_per_step)],
        out_specs=pl.BlockSpec((1, bps, nh, dh), lambda b, s, pt: (b, s, 0, 0)))
    return pl.pallas_call(
        functools.partial(_pagesum_kernel, pages_per_block=ppb),
        grid_spec=grid_spec,
        out_shape=jax.ShapeDtypeStruct((db, n_pages // ppb, nh, dh), F32),
        compiler_params=_cparams(("arbitrary", "arbitrary")),
        name="page_block_sums",
    )(page_table, *([cache_k] * pages_per_step))


def _merge_kernel(x_ref, oa_ref, oc_ref, om_ref, gt_ref, wa_ref, wc_ref, wm_ref, wo_ref, g_ref, wr_ref, br_ref,
                  h_ref, hn_ref, te_ref, tw_ref):
    d = x_ref.shape[1]
    tm = x_ref.shape[0]
    merged = (gt_ref[:, 0:d] * _dot(oa_ref[...], wa_ref[...])
              + gt_ref[:, d:2 * d] * _dot(oc_ref[...], wc_ref[...])
              + gt_ref[:, 2 * d:3 * d] * _dot(om_ref[...], wm_ref[...]))
    h = x_ref[...] + _dot(merged.astype(BF16), wo_ref[...])
    h_ref[...] = h
    hn = _rms(h, g_ref[...])
    hi, lo = _split_bf16(hn)
    hn_ref[...] = hi
    whi, wlo = _split_bf16(wr_ref[...])
    logits = _dot(hi, whi) + _dot(lo, whi) + _dot(hi, wlo) + br_ref[...]
    ne = logits.shape[1]
    lane = lax.broadcasted_iota(I32, (tm, ne), 1)
    out_lane = lax.broadcasted_iota(I32, (tm, LANES), 1)
    s = logits
    vals, te = [], jnp.zeros((tm, LANES), I32)
    for t in range(TOP_K):
        mx = jnp.max(s, axis=-1, keepdims=True)
        idx = jnp.min(jnp.where(s == mx, lane, ne), axis=-1, keepdims=True)
        vals.append(mx)
        te = jnp.where(out_lane == t, idx, te)
        s = jnp.where(lane == idx, -jnp.inf, s)
    es = [jnp.exp(v - vals[0]) for v in vals]
    inv = 1.0 / functools.reduce(lambda a, b: a + b, es)
    tw = jnp.zeros((tm, LANES), F32)
    for t in range(TOP_K):
        tw = jnp.where(out_lane == t, es[t] * inv, tw)
    te_ref[...] = te
    tw_ref[...] = tw


def _merge(x2d, oa, oc, om, gates, wa, wc, wm, wo, g, wr, br, *, tm):
    r, d = x2d.shape
    ne = wr.shape[1]
    row = lambda width: pl.BlockSpec((tm, width), lambda i: (i, 0))
    return pl.pallas_call(
        _merge_kernel,
        grid=(r // tm,),
        in_specs=[row(d), row(oa.shape[1]), row(oc.shape[1]), row(om.shape[1]), row(gates.shape[1]),
                  _const_spec(wa.shape), _const_spec(wc.shape), _const_spec(wm.shape), _const_spec(wo.shape),
                  _const_spec((1, d)), _const_spec(wr.shape), _const_spec((1, ne))],
        out_specs=[row(d), row(d), row(LANES), row(LANES)],
        out_shape=[jax.ShapeDtypeStruct((r, d), F32), jax.ShapeDtypeStruct((r, d), BF16),
                   jax.ShapeDtypeStruct((r, LANES), I32), jax.ShapeDtypeStruct((r, LANES), F32)],
        compiler_params=_cparams(("arbitrary",)),
        name="merge",
    )(x2d, oa, oc, om, gates, wa, wc, wm, wo, g.reshape(1, d), wr, br.reshape(1, ne))


def _moe_kernel(be_ref, nu_ref, xs_ref, wgu_ref, bgu_ref, wd_ref, bd_ref, o_ref, wgu_b, wd_b):
    i = pl.program_id(0)
    e = be_ref[i]
    prev = be_ref[jnp.maximum(i - 1, 0)]

    @pl.when((i == 0) | (e != prev))
    def _():
        wgu_b[...] = wgu_ref[0].astype(BF16)
        wd_b[...] = wd_ref[0].astype(BF16)

    @pl.when(i < nu_ref[0])
    def _():
        f = wd_b.shape[0]
        gu = _dot(xs_ref[...], wgu_b[...]) + bgu_ref[0]
        g = jnp.minimum(gu[:, :f], SWIGLU_LIMIT)
        u = jnp.clip(gu[:, f:], -SWIGLU_LIMIT, SWIGLU_LIMIT)
        hdn = g * jax.nn.sigmoid(SWIGLU_ALPHA * g) * (u + 1.0)
        o_ref[...] = _dot(hdn.astype(BF16), wd_b[...]) + bd_ref[0]

    @pl.when(i >= nu_ref[0])
    def _():
        o_ref[...] = jnp.zeros(o_ref.shape, F32)


def _moe_experts(xs, blk_e, n_used, w_gate_up, b_gate_up, w_down, b_down):
    n_rows, d = xs.shape
    ne, _, f2 = w_gate_up.shape
    f = f2 // 2
    te = MOE_ROW_TILE
    grid_spec = pltpu.PrefetchScalarGridSpec(
        num_scalar_prefetch=2,
        grid=(n_rows // te,),
        in_specs=[pl.BlockSpec((te, d), lambda i, be, nu: (i, 0)),
                  pl.BlockSpec((1, d, f2), lambda i, be, nu: (be[i], 0, 0)),
                  pl.BlockSpec((1, 1, f2), lambda i, be, nu: (be[i], 0, 0)),
                  pl.BlockSpec((1, f, d), lambda i, be, nu: (be[i], 0, 0)),
                  pl.BlockSpec((1, 1, d), lambda i, be, nu: (be[i], 0, 0))],
        out_specs=pl.BlockSpec((te, d), lambda i, be, nu: (i, 0)),
        scratch_shapes=[pltpu.VMEM((d, f2), BF16), pltpu.VMEM((f, d), BF16)])
    return pl.pallas_call(
        _moe_kernel,
        grid_spec=grid_spec,
        out_shape=jax.ShapeDtypeStruct((n_rows, d), F32),
        compiler_params=_cparams(("arbitrary",)),
        name="moe_experts",
    )(blk_e, n_used, xs, w_gate_up, b_gate_up.reshape(ne, 1, f2), w_down, b_down.reshape(ne, 1, d))


def _moe_plan(top_e, n_tok, ne):
    ne_tile = MOE_ROW_TILE
    onehot = (top_e[:, :, None] == jnp.arange(ne, dtype=I32)[None, None, :]).astype(I32).sum(axis=1)
    incl = jnp.cumsum(onehot, axis=0)
    counts = incl[-1]
    rank = jnp.take_along_axis(incl - onehot, top_e, axis=1)
    padded = (counts + ne_tile - 1) // ne_tile * ne_tile
    pad_end = jnp.cumsum(padded)
    pad_start = pad_end - padded
    dest = pad_start[top_e] + rank
    n_tiles = (n_tok * TOP_K) // ne_tile + ne
    blk_e = jnp.minimum(jnp.searchsorted(pad_end, jnp.arange(n_tiles, dtype=I32) * ne_tile, side='right'),
                        ne - 1).astype(I32)
    n_used = (pad_end[-1] // ne_tile).astype(I32).reshape(1)
    return dest.astype(I32), blk_e, n_used, n_tiles * ne_tile


def _combine_kernel(h_ref, rows_ref, tw_ref, g_ref, y_ref):
    tw = tw_ref[...]
    h = h_ref[...]
    for t in range(rows_ref.shape[0]):
        h = h + tw[:, t:t + 1] * rows_ref[t]
    y_ref[...] = _rms(h, g_ref[...])


def _combine(h, rows, tw, g, *, tm):
    r, d = h.shape
    k = rows.shape[0]
    return pl.pallas_call(
        _combine_kernel,
        grid=(r // tm,),
        in_specs=[pl.BlockSpec((tm, d), lambda i: (i, 0)), pl.BlockSpec((k, tm, d), lambda i: (0, i, 0)),
                  pl.BlockSpec((tm, LANES), lambda i: (i, 0)), _const_spec((1, d))],
        out_specs=pl.BlockSpec((tm, d), lambda i: (i, 0)),
        out_shape=jax.ShapeDtypeStruct((r, d), F32),
        compiler_params=_cparams(("arbitrary",)),
        name="combine",
    )(h, rows, tw, g.reshape(1, d))


def kernel(x_prompt, x_sample, mem_prompt, cache_k, cache_v, page_table, state_conv, cache_mem_k, cache_mem_v,
           norm_mix_g, w_in, w_dw, b_dw, conv_ln_g, conv_ln_b, norm_mem_g, w_mem_kv, w_branch_a, w_branch_c,
           w_branch_m, w_out, norm_ffn_g, w_router, b_router, w_gate_up, b_gate_up, w_down, b_down, norm_final_g):
    bp, seq, d = x_prompt.shape
    db, dseq, _ = x_sample.shape
    assert bp == 1 and dseq == 1
    n_mem, m_heads, m_hd = cache_mem_k.shape[1:]
    mw = m_heads * m_hd
    m_scale = m_hd ** -0.5
    cw = w_dw.shape[1]
    kw = w_dw.shape[0]
    page = cache_k.shape[1]
    n_pages = page_table.shape[1]
    past = n_pages * page
    assert past % MOBA_BLOCK == 0 and cache_k.shape[2:] == (A_HEADS, A_HEAD_DIM)

    w_in_bf = w_in.astype(BF16)
    wa, wc, wm, wo = (w.astype(BF16) for w in (w_branch_a, w_branch_c, w_branch_m, w_out))

    xp = x_prompt.reshape(seq, d)
    kv_mem = _memory_kv(mem_prompt.reshape(n_mem, d), norm_mem_g, w_mem_kv)
    mk_p, mv_p = kv_mem[:, :mw], kv_mem[:, mw:]
    cos_p, sin_p = _rope_tables(jnp.arange(seq, dtype=I32))
    q_p, kf_p, kb_p, vf_p, vb_p, hg_p, qm_p, gt_p, ksum_p = _in_projection(
        xp, norm_mix_g, w_in_bf, cos_p, sin_p, tm=512, cw=cw, mw=mw, m_scale=m_scale, block_sums=True)
    oa_p = _moba_prompt(q_p, kb_p, vb_p, ksum_p.reshape(seq // MOBA_BLOCK, A_WIDTH), tq=256)
    oc_p = _conv_prompt(hg_p, jnp.zeros((kw - 1, cw), F32), w_dw, b_dw, conv_ln_g, conv_ln_b,
                        tm=_row_tile(seq, 512))
    om_p = _memattn_prompt(qm_p, mk_p.astype(BF16), mv_p.astype(BF16), tm=_row_tile(seq, 512), hd=m_hd)
    h_p, hn_p, te_p, tw_p = _merge(xp, oa_p, oc_p, om_p, gt_p, wa, wc, wm, wo, norm_ffn_g, w_router, b_router,
                                   tm=_row_tile(seq, 256))

    xs = x_sample.reshape(db, d)
    cos_s, sin_s = _rope_tables(jnp.full((db,), past, I32))
    q_s, kf_s, _, vf_s, _, hg_s, qm_s, gt_s = _in_projection(
        xs, norm_mix_g, w_in_bf, cos_s, sin_s, tm=db, cw=cw, mw=mw, m_scale=m_scale, block_sums=False)
    ksum_s = _page_block_sums(cache_k, page_table, pages_per_step=8)
    sc = jnp.einsum('bhd,bnhd->bhn', q_s.astype(F32).reshape(db, A_HEADS, A_HEAD_DIM), ksum_s,
                    precision=lax.Precision.HIGHEST)
    _, top = lax.top_k(sc, MOBA_TOPK)
    ppb = MOBA_BLOCK // page
    pages = page_table[jnp.arange(db)[:, None, None, None],
                       top[..., None] * ppb + jnp.arange(ppb)[None, None, None, :]]
    h_i = jnp.arange(A_HEADS)[None, :, None, None]

    def fetch(cache):
        sel = cache[pages, :, h_i, :]
        sel = sel.reshape(db, A_HEADS, MOBA_TOPK * MOBA_BLOCK, A_HEAD_DIM)
        return sel.transpose(0, 2, 1, 3).reshape(db, MOBA_TOPK * MOBA_BLOCK, A_WIDTH)

    oa_s = _decode_attend(q_s, fetch(cache_k), fetch(cache_v), kf_s, vf_s, hd=A_HEAD_DIM)
    oc_s = _conv_step(state_conv.transpose(1, 0, 2), hg_s, w_dw, b_dw, conv_ln_g, conv_ln_b)
    om_s = _decode_attend(qm_s, cache_mem_k.reshape(db, n_mem, mw), cache_mem_v.reshape(db, n_mem, mw), hd=m_hd)
    h_s, hn_s, te_s, tw_s = _merge(xs, oa_s, oc_s, om_s, gt_s, wa, wc, wm, wo, norm_ffn_g, w_router, b_router, tm=db)

    n_tok = seq + db
    hn = jnp.concatenate([hn_p, hn_s], axis=0)
    top_e = jnp.concatenate([te_p[:, :TOP_K], te_s[:, :TOP_K]], axis=0)
    tw = jnp.concatenate([tw_p, tw_s], axis=0)
    h_all = jnp.concatenate([h_p, h_s], axis=0)
    dest, blk_e, n_used, n_rows = _moe_plan(top_e, n_tok, w_router.shape[1])
    row_tok = jnp.zeros((n_rows,), I32).at[dest.reshape(-1)].set(
        jnp.repeat(jnp.arange(n_tok, dtype=I32), TOP_K))
    out_rows = _moe_experts(hn[row_tok], blk_e, n_used, w_gate_up, b_gate_up, w_down, b_down)
    y = _combine(h_all, out_rows[dest.T], tw, norm_final_g, tm=_row_tile(n_tok, 384))

    y_prompt = y[:seq].reshape(1, seq, d)
    y_sample = y[seq:].reshape(db, 1, d)
    k_prompt = kf_p.reshape(1, seq, A_HEADS, A_HEAD_DIM)
    v_prompt = vf_p.reshape(1, seq, A_HEADS, A_HEAD_DIM)
    conv_prompt = hg_p[seq - (kw - 1):].reshape(1, kw - 1, cw)
    mem_k_prompt = mk_p.reshape(1, n_mem, m_heads, m_hd)
    mem_v_prompt = mv_p.reshape(1, n_mem, m_heads, m_hd)
    k_sample = kf_s.reshape(db, 1, A_HEADS, A_HEAD_DIM)
    v_sample = vf_s.reshape(db, 1, A_HEADS, A_HEAD_DIM)
    conv_sample = jnp.concatenate([state_conv[:, 1:], hg_s[:, None, :]], axis=1)
    return (y_prompt, y_sample, k_prompt, v_prompt, conv_prompt, mem_k_prompt, mem_v_prompt,
            k_sample, v_sample, conv_sample)
```

```python
import functools
import math

import numpy as np
import jax
import jax.numpy as jnp
from jax import lax
from jax.experimental import pallas as pl
from jax.experimental.pallas import tpu as pltpu

F32 = jnp.float32
BF16 = jnp.bfloat16
I32 = jnp.int32

A_HEADS = 8
A_HEAD_DIM = 64
A_WIDTH = A_HEADS * A_HEAD_DIM
A_SCALE = A_HEAD_DIM ** -0.5
MOBA_BLOCK = 256
MOBA_TOPK = 3
ROPE_THETA = 10000.0
N_BRANCH = 3
TOP_K = 4
SWIGLU_LIMIT = 7.0
SWIGLU_ALPHA = 1.702
EPS = 1e-5
NEG = -1e30
LOG2E = math.log2(math.e)

LANES = 128
VMEM_LIMIT = 56 * 1024 * 1024
MOE_ROW_TILE = 256
V_AUG = A_HEAD_DIM + 16


def _cparams(sem):
    return pltpu.CompilerParams(dimension_semantics=sem, vmem_limit_bytes=VMEM_LIMIT)


def _row_tile(n, cap):
    t = min(cap, n) // 8 * 8
    while n % t:
        t -= 8
    return t


def _const_spec(shape):
    nd = len(shape)
    return pl.BlockSpec(shape, lambda *_: (0,) * nd, pipeline_mode=pl.Buffered(1))


def _rms(x, g):
    return x * lax.rsqrt(jnp.mean(x * x, axis=-1, keepdims=True) + EPS) * g


def _split_bf16(x):
    hi = x.astype(BF16)
    lo = (x - hi.astype(F32)).astype(BF16)
    return hi, lo


def _dot(a, b):
    return jnp.dot(a, b, preferred_element_type=F32)


def _dot_t(a, b):
    return lax.dot_general(a, b, (((1,), (1,)), ((), ())), preferred_element_type=F32)


def _memkv_kernel(mem_ref, g_ref, w_ref, kv_ref):
    xn = _rms(mem_ref[...], g_ref[...])
    kv_ref[...] = _dot(xn.astype(BF16), w_ref[...].astype(BF16))


def _memory_kv(mem2d, g, w):
    n, d = mem2d.shape
    return pl.pallas_call(
        _memkv_kernel,
        out_shape=jax.ShapeDtypeStruct((n, w.shape[1]), F32),
        compiler_params=_cparams(None),
        name="memory_kv",
    )(mem2d, g.reshape(1, d), w)


def _inproj_kernel(x_ref, g_ref, w_ref, cos_ref, sin_ref, q_ref, kf_ref, vf_ref, h_ref, qm_ref, gt_ref,
                   *prompt_refs, m_scale):
    tm = x_ref.shape[0]
    aw = q_ref.shape[1]
    cw = h_ref.shape[1]
    mw = qm_ref.shape[1]
    xb = _rms(x_ref[...], g_ref[...]).astype(BF16)

    def proj(c0, n):
        return _dot(xb, w_ref[:, c0:c0 + n])

    cos = cos_ref[...]
    sin = sin_ref[...]
    lane = lax.broadcasted_iota(I32, (tm, LANES), 1)
    first_half = (lane % A_HEAD_DIM) < (A_HEAD_DIM // 2)

    def rope(c):
        swapped = jnp.where(first_half, pltpu.roll(c, LANES - A_HEAD_DIM // 2, 1),
                            pltpu.roll(c, A_HEAD_DIM // 2, 1))
        return c * cos + swapped * sin

    q = proj(0, aw)
    k = proj(aw, aw)
    for j in range(aw // LANES):
        sl = slice(j * LANES, (j + 1) * LANES)
        q_ref[:, sl] = rope(q[:, sl]).astype(BF16)
        kf_ref[:, sl] = rope(k[:, sl])
    v = proj(2 * aw, aw)
    vf_ref[...] = v
    if prompt_refs:
        kb_ref, vt_ref, ks_ref = prompt_refs
        kb_ref[...] = (kf_ref[...] * (A_SCALE * LOG2E)).astype(BF16)
        for b in range(tm // MOBA_BLOCK):
            ks_ref[b] = jnp.sum(kf_ref[b * MOBA_BLOCK:(b + 1) * MOBA_BLOCK, :], axis=0, keepdims=True)
        vt = v.T
        ones = jnp.ones((V_AUG - A_HEAD_DIM, tm), BF16)
        for h in range(A_HEADS):
            vt_ref[h * V_AUG:h * V_AUG + A_HEAD_DIM, :] = vt[h * A_HEAD_DIM:(h + 1) * A_HEAD_DIM, :].astype(BF16)
            vt_ref[h * V_AUG + A_HEAD_DIM:(h + 1) * V_AUG, :] = ones
    a = proj(3 * aw, cw)
    gate = proj(3 * aw + cw, cw)
    h_ref[...] = a * jax.nn.sigmoid(gate)
    c0 = 3 * aw + 2 * cw
    qm_ref[...] = (proj(c0, mw) * m_scale).astype(BF16)
    c0 += mw
    gw = gt_ref.shape[1]
    step = 512
    for j in range(gw // step):
        gt_ref[:, j * step:(j + 1) * step] = jax.nn.sigmoid(proj(c0 + j * step, step))


def _in_projection(x2d, g, w_bf, cos, sin, *, tm, cw, mw, m_scale, prompt):
    r, d = x2d.shape
    aw = A_WIDTH
    gw = w_bf.shape[1] - (3 * aw + 2 * cw + mw)
    row = lambda width: pl.BlockSpec((tm, width), lambda i: (i, 0))
    out_shape = [jax.ShapeDtypeStruct((r, aw), BF16), jax.ShapeDtypeStruct((r, aw), F32),
                 jax.ShapeDtypeStruct((r, aw), F32), jax.ShapeDtypeStruct((r, cw), F32),
                 jax.ShapeDtypeStruct((r, mw), BF16), jax.ShapeDtypeStruct((r, gw), F32)]
    out_specs = [row(aw), row(aw), row(aw), row(cw), row(mw), row(gw)]
    if prompt:
        assert tm % MOBA_BLOCK == 0
        bpt = tm // MOBA_BLOCK
        out_shape += [jax.ShapeDtypeStruct((r, aw), BF16), jax.ShapeDtypeStruct((A_HEADS * V_AUG, r), BF16),
                      jax.ShapeDtypeStruct((r // MOBA_BLOCK, 1, aw), F32)]
        out_specs += [row(aw), pl.BlockSpec((A_HEADS * V_AUG, tm), lambda i: (0, i)),
                      pl.BlockSpec((bpt, 1, aw), lambda i: (i, 0, 0))]
    return pl.pallas_call(
        functools.partial(_inproj_kernel, m_scale=m_scale),
        grid=(r // tm,),
        in_specs=[row(d), _const_spec((1, d)), _const_spec(w_bf.shape), row(LANES), row(LANES)],
        out_specs=out_specs,
        out_shape=out_shape,
        compiler_params=_cparams(("arbitrary",)),
        name="in_projection",
    )(x2d, g.reshape(1, d), w_bf, cos, sin)


def _rope_tables(pos):
    half = A_HEAD_DIM // 2
    inv = ROPE_THETA ** (-jnp.arange(half, dtype=F32) / half)
    ang = pos.astype(F32)[:, None] * inv[None, :]
    cos = jnp.cos(ang)
    sin = jnp.sin(ang)
    reps = LANES // A_HEAD_DIM
    return (jnp.tile(jnp.concatenate([cos, cos], axis=-1), (1, reps)),
            jnp.tile(jnp.concatenate([-sin, sin], axis=-1), (1, reps)))


def _moba_kernel(qi_ref, ki_ref, q_ref, k_ref, vt_ref, ks_ref, o_ref, qaug, m_scr, acc, st_scr, *, qt):
    step = pl.program_id(0)
    qi = qi_ref[step]
    kb = ki_ref[step]
    dh = A_HEAD_DIM
    blk = MOBA_BLOCK
    n_chunks = A_WIDTH // LANES

    @pl.when(kb == 0)
    def _select():
        m_scr[...] = jnp.full(m_scr.shape, -jnp.inf, F32)
        acc[...] = jnp.zeros(acc.shape, F32)
        lane_k = lax.broadcasted_iota(I32, (dh, LANES), 1)
        rowid = lax.broadcasted_iota(I32, (dh, blk), 0)
        for s in range(qt):
            j = qi * qt + s
            qT = q_ref[s * blk:(s + 1) * blk, :].astype(F32).T
            for c in range(n_chunks):
                qTc = qT[c * LANES:(c + 1) * LANES, :].astype(BF16)
                ksc = ks_ref[:, c * LANES:(c + 1) * LANES]
                for par in range(2):
                    h = 2 * c + par
                    mine = (lane_k < dh) if par == 0 else (lane_k >= dh)
                    sc = _dot(jnp.where(mine, ksc, 0.0).astype(BF16), qTc)
                    sc = jnp.where(rowid < j, sc, NEG)
                    sel = rowid == j
                    for t in range(MOBA_TOPK):
                        mx = jnp.max(sc, axis=0, keepdims=True)
                        idx = jnp.min(jnp.where(sc == mx, rowid, dh), axis=0, keepdims=True)
                        pick = rowid == idx
                        sel = sel | (pick & (t < j))
                        sc = jnp.where(pick, -jnp.inf, sc)
                    bias = jnp.where(sel, 0.0, NEG).astype(BF16)
                    qh = qTc[par * dh:(par + 1) * dh, :]
                    qaug[s * A_HEADS + h, par * dh:(par + 1) * dh, :] = qh
                    qaug[s * A_HEADS + h, (1 - par) * dh:(2 - par) * dh, :] = bias

    def sweep(subs, masked):
        lane_k = lax.broadcasted_iota(I32, (blk, LANES), 1)
        keep = [jnp.where(lane_k < dh, 1.0, 0.0).astype(BF16), jnp.where(lane_k >= dh, 1.0, 0.0).astype(BF16)]
        onehot = [jnp.where(lane_k - dh == kb, 1.0, 0.0).astype(BF16), jnp.where(lane_k == kb, 1.0, 0.0).astype(BF16)]
        if masked:
            causal = lax.broadcasted_iota(I32, (blk, blk), 0) <= lax.broadcasted_iota(I32, (blk, blk), 1)
        chains = [(s, h) for s in subs for h in range(A_HEADS)]
        slots = st_scr.shape[0]

        def logits(i):
            s, h = chains[i]
            c, par = divmod(h, 2)
            ka = k_ref[:, c * LANES:(c + 1) * LANES] * keep[par] + onehot[par]
            st_scr[i % slots] = _dot(ka, qaug[s * A_HEADS + h])

        for i in range(min(slots - 1, len(chains))):
            logits(i)
        for i, (s, h) in enumerate(chains):
            if i + slots - 1 < len(chains):
                logits(i + slots - 1)
            a = s * A_HEADS + h
            st = st_scr[i % slots]
            if masked:
                st = jnp.where(causal, st, NEG)
            m_prev = m_scr[s, h:h + 1, :]
            m_new = jnp.maximum(m_prev, jnp.max(st, axis=0, keepdims=True))
            p = jnp.exp2(st - m_new).astype(BF16)
            acc[a] = jnp.exp2(m_prev - m_new) * acc[a] + _dot(vt_ref[h * V_AUG:(h + 1) * V_AUG, :], p)
            m_scr[s, h:h + 1, :] = m_new

    pl.when(kb < qi * qt)(functools.partial(sweep, list(range(qt)), False))
    for s in range(qt):
        j = qi * qt + s
        pl.when((kb >= qi * qt) & (kb < j))(functools.partial(sweep, [s], False))
        pl.when(kb == j)(functools.partial(sweep, [s], True))

    @pl.when(kb == qi * qt + qt - 1)
    def _finish():
        for s in range(qt):
            outs = []
            for h in range(A_HEADS):
                a = acc[s * A_HEADS + h]
                outs.append(a[0:dh, :] / a[dh:dh + 1, :])
            o_ref[s * blk:(s + 1) * blk, :] = jnp.concatenate(outs, axis=0).T.astype(BF16)


def _moba_prompt(q, kb, vt, ksum, *, qt):
    r, aw = q.shape
    nb = r // MOBA_BLOCK
    assert r % (qt * MOBA_BLOCK) == 0 and MOBA_TOPK <= nb <= A_HEAD_DIM and A_HEAD_DIM * 2 == LANES
    tq = qt * MOBA_BLOCK
    ks = jnp.zeros((A_HEAD_DIM, aw), F32).at[:nb].set(ksum)
    qi, ki = [], []
    for i in range(r // tq):
        for n in range((i + 1) * qt):
            qi.append(i)
            ki.append(n)
    grid_spec = pltpu.PrefetchScalarGridSpec(
        num_scalar_prefetch=2,
        grid=(len(qi),),
        in_specs=[pl.BlockSpec((tq, aw), lambda s, qi, ki: (qi[s], 0)),
                  pl.BlockSpec((MOBA_BLOCK, aw), lambda s, qi, ki: (ki[s], 0)),
                  pl.BlockSpec((A_HEADS * V_AUG, MOBA_BLOCK), lambda s, qi, ki: (0, ki[s])),
                  pl.BlockSpec(ks.shape, lambda s, qi, ki: (0, 0))],
        out_specs=pl.BlockSpec((tq, aw), lambda s, qi, ki: (qi[s], 0)),
        scratch_shapes=[pltpu.VMEM((qt * A_HEADS, LANES, MOBA_BLOCK), BF16),
                        pltpu.VMEM((qt, A_HEADS, MOBA_BLOCK), F32),
                        pltpu.VMEM((qt * A_HEADS, V_AUG, MOBA_BLOCK), F32),
                        pltpu.VMEM((9, MOBA_BLOCK, MOBA_BLOCK), F32)])
    return pl.pallas_call(
        functools.partial(_moba_kernel, qt=qt),
        grid_spec=grid_spec,
        out_shape=jax.ShapeDtypeStruct((r, aw), BF16),
        compiler_params=_cparams(("arbitrary",)),
        name="moba_prompt",
    )(jnp.asarray(np.array(qi, np.int32)), jnp.asarray(np.array(ki, np.int32)), q, kb, vt, ks)


def _ln_silu(y, g, b):
    mu = jnp.mean(y, axis=-1, keepdims=True)
    d = y - mu
    var = jnp.mean(d * d, axis=-1, keepdims=True)
    z = d * lax.rsqrt(var + EPS) * g + b
    return z * jax.nn.sigmoid(z)


def _conv_prompt_kernel(h_ref, st_ref, w_ref, b_ref, g_ref, beta_ref, o_ref, ext, *, halo, rc):
    tm = h_ref.shape[0]
    kw = w_ref.shape[0]

    @pl.when(pl.program_id(0) == 0)
    def _():
        ext[0:halo, :] = st_ref[...]

    ext[halo:halo + tm, :] = h_ref[...]
    off = halo - (kw - 1)
    for r0 in range(0, tm, rc):
        y = jnp.zeros((rc, h_ref.shape[1]), F32) + b_ref[...]
        for k in range(kw):
            y = y + w_ref[k:k + 1, :] * ext[r0 + off + k:r0 + off + k + rc, :]
        o_ref[r0:r0 + rc, :] = _ln_silu(y, g_ref[...], beta_ref[...]).astype(BF16)
    ext[0:halo, :] = ext[tm:tm + halo, :]


def _conv_prompt(h2d, state, w_dw, b_dw, ln_g, ln_b, *, tm):
    r, c = h2d.shape
    kw = w_dw.shape[0]
    halo = -(-(kw - 1) // 8) * 8
    st = jnp.zeros((halo, c), F32).at[halo - (kw - 1):].set(state)
    vec = lambda a: a.reshape(1, c)
    return pl.pallas_call(
        functools.partial(_conv_prompt_kernel, halo=halo, rc=64),
        grid=(r // tm,),
        in_specs=[pl.BlockSpec((tm, c), lambda i: (i, 0)), _const_spec((halo, c)), _const_spec((kw, c)),
                  _const_spec((1, c)), _const_spec((1, c)), _const_spec((1, c))],
        out_specs=pl.BlockSpec((tm, c), lambda i: (i, 0)),
        out_shape=jax.ShapeDtypeStruct((r, c), BF16),
        scratch_shapes=[pltpu.VMEM((tm + halo, c), F32)],
        compiler_params=_cparams(("arbitrary",)),
        name="conv_prompt",
    )(h2d, st, w_dw, vec(b_dw), vec(ln_g), vec(ln_b))


def _conv_step_kernel(st_ref, h_ref, w_ref, b_ref, g_ref, beta_ref, o_ref):
    kw = w_ref.shape[0]
    y = b_ref[...] + w_ref[kw - 1:kw, :] * h_ref[...]
    for k in range(kw - 1):
        y = y + w_ref[k:k + 1, :] * st_ref[k]
    o_ref[...] = _ln_silu(y, g_ref[...], beta_ref[...]).astype(BF16)


def _conv_step(state_t, h2d, w_dw, b_dw, ln_g, ln_b):
    b, c = h2d.shape
    vec = lambda a: a.reshape(1, c)
    return pl.pallas_call(
        _conv_step_kernel,
        out_shape=jax.ShapeDtypeStruct((b, c), BF16),
        compiler_params=_cparams(None),
        name="conv_step",
    )(state_t, h2d, w_dw, vec(b_dw), vec(ln_g), vec(ln_b))


def _memattn_kernel(q_ref, k_ref, v_ref, o_ref, *, hd):
    for h in range(q_ref.shape[1] // hd):
        sl = slice(h * hd, (h + 1) * hd)
        s = _dot_t(q_ref[:, sl], k_ref[:, sl])
        p = jnp.exp(s - jnp.max(s, axis=-1, keepdims=True))
        inv = 1.0 / jnp.sum(p, axis=-1, keepdims=True)
        o_ref[:, sl] = (_dot(p.astype(BF16), v_ref[:, sl]) * inv).astype(BF16)


def _memattn_prompt(qm, mk_bf, mv_bf, *, tm, hd):
    r, w = qm.shape
    return pl.pallas_call(
        functools.partial(_memattn_kernel, hd=hd),
        grid=(r // tm,),
        in_specs=[pl.BlockSpec((tm, w), lambda i: (i, 0)), _const_spec(mk_bf.shape), _const_spec(mv_bf.shape)],
        out_specs=pl.BlockSpec((tm, w), lambda i: (i, 0)),
        out_shape=jax.ShapeDtypeStruct((r, w), BF16),
        compiler_params=_cparams(("arbitrary",)),
        name="memattn_prompt",
    )(qm, mk_bf, mv_bf)


def _memattn_step_kernel(q_ref, k_ref, v_ref, o_ref, *, hd):
    w = q_ref.shape[2]
    head_of_lane = lax.broadcasted_iota(I32, (LANES, w), 1) // hd
    head_row = lax.broadcasted_iota(I32, (LANES, w), 0)
    own = head_of_lane == head_row
    expand = jnp.where(own, 1.0, 0.0).astype(BF16)
    q_heads = jnp.where(own, q_ref[0].astype(F32), 0.0).astype(BF16)
    s = _dot_t(k_ref[0].astype(BF16), q_heads)
    m = jnp.max(s, axis=0, keepdims=True)
    pe = _dot(jnp.exp(s - m).astype(BF16), expand)
    num = jnp.sum(pe * v_ref[0], axis=0, keepdims=True)
    den = jnp.sum(pe, axis=0, keepdims=True)
    o_ref[0] = (num / den).astype(BF16)


def _memattn_step(q, k, v, *, hd):
    b, w = q.shape
    nk = k.shape[1]
    assert w // hd <= LANES
    one = pl.BlockSpec((1, 1, w), lambda i: (i, 0, 0))
    many = pl.BlockSpec((1, nk, w), lambda i: (i, 0, 0))
    out = pl.pallas_call(
        functools.partial(_memattn_step_kernel, hd=hd),
        grid=(b,),
        in_specs=[one, many, many],
        out_specs=one,
        out_shape=jax.ShapeDtypeStruct((b, 1, w), BF16),
        compiler_params=_cparams(("arbitrary",)),
        name="memattn_step",
    )(q.reshape(b, 1, w), k, v)
    return out.reshape(b, w)


def _block_select_kernel(pt_ref, q_ref, *refs, pages_per_block, n_blocks):
    del pt_ref
    page_refs = refs[:-2]
    top_ref, sc = refs[-2:]
    s = pl.program_id(1)
    nh, dh = page_refs[0].shape[1:3]
    bps = len(page_refs) // pages_per_block
    q = q_ref[0]
    for m in range(bps):
        tot = page_refs[m * pages_per_block][0]
        for j in range(1, pages_per_block):
            tot = tot + page_refs[m * pages_per_block + j][0]
        ksum = jnp.sum(tot.reshape(nh * dh, tot.shape[-1]), axis=-1, keepdims=True)
        kmean = (ksum * (1.0 / MOBA_BLOCK)).astype(BF16).astype(F32)
        sc[s * bps + m] = jnp.sum((kmean * q).reshape(nh, dh, 1), axis=1)

    @pl.when(s == pl.num_programs(1) - 1)
    def _():
        scores = sc[...]
        ids = lax.broadcasted_iota(I32, scores.shape, 0)
        for t in range(MOBA_TOPK):
            mx = jnp.max(scores, axis=0, keepdims=True)
            idx = jnp.min(jnp.where(scores == mx, ids, n_blocks), axis=0, keepdims=True)
            top_ref[0, t] = idx[0]
            scores = jnp.where(ids == idx, -jnp.inf, scores)


def _block_select(cache_kt, page_table, q_col, *, pages_per_step):
    db, n_pages = page_table.shape
    _, nh, dh, page = cache_kt.shape
    ppb = MOBA_BLOCK // page
    n_blocks = n_pages // ppb
    assert n_pages % pages_per_step == 0 and pages_per_step % ppb == 0 and n_blocks >= MOBA_TOPK

    def page_spec(j):
        return pl.BlockSpec((1, nh, dh, page), lambda b, s, pt: (pt[b, s * pages_per_step + j], 0, 0, 0))

    grid_spec = pltpu.PrefetchScalarGridSpec(
        num_scalar_prefetch=1,
        grid=(db, n_pages // pages_per_step),
        in_specs=[pl.BlockSpec((1, nh * dh, 1), lambda b, s, pt: (b, 0, 0))]
        + [page_spec(j) for j in range(pages_per_step)],
        out_specs=pl.BlockSpec((1, MOBA_TOPK, nh, 1), lambda b, s, pt: (b, 0, 0, 0)),
        scratch_shapes=[pltpu.VMEM((n_blocks, nh, 1), F32)])
    return pl.pallas_call(
        functools.partial(_block_select_kernel, pages_per_block=ppb, n_blocks=n_blocks),
        grid_spec=grid_spec,
        out_shape=jax.ShapeDtypeStruct((db, MOBA_TOPK, nh, 1), I32),
        compiler_params=_cparams(("arbitrary", "arbitrary")),
        name="block_select",
    )(page_table, q_col, *([cache_kt] * pages_per_step))


def _moba_step_kernel(pg_ref, q_ref, kn_ref, vn_ref, *refs, n_tiles):
    del pg_ref
    o_ref = refs[-1]
    k_refs = refs[:A_HEADS * n_tiles]
    v_refs = refs[A_HEADS * n_tiles:2 * A_HEADS * n_tiles]
    dh = A_HEAD_DIM
    for h in range(A_HEADS):
        rows = slice(h * dh, (h + 1) * dh)
        q = q_ref[0, rows, :]
        s_new = jnp.sum(q * kn_ref[0, rows, :], axis=0, keepdims=True) * A_SCALE
        ss = [jnp.sum(k_refs[h * n_tiles + t][0, 0] * q, axis=0, keepdims=True) * A_SCALE
              for t in range(n_tiles)]
        m = s_new
        for s in ss:
            m = jnp.maximum(m, jnp.max(s, axis=-1, keepdims=True))
        p_new = jnp.exp(s_new - m)
        den = p_new
        num = p_new * vn_ref[0, rows, :]
        acc = None
        for t in range(n_tiles):
            p = jnp.exp(ss[t] - m)
            den = den + jnp.sum(p, axis=-1, keepdims=True)
            pv = v_refs[h * n_tiles + t][0, 0] * p
            acc = pv if acc is None else acc + pv
        num = num + jnp.sum(acc, axis=-1, keepdims=True)
        o_ref[0, rows, :] = num / den


def _moba_step(cache_kt, cache_vt, pages, q_col, kn_col, vn_col):
    db, w, _ = q_col.shape
    _, nh, dh, page = cache_kt.shape
    n_tiles = pages.shape[0] // (db * nh)
    col = pl.BlockSpec((1, w, 1), lambda b, pg: (b, 0, 0))

    def tile_spec(h, t):
        return pl.BlockSpec((1, 1, dh, page), lambda b, pg: (pg[(b * nh + h) * n_tiles + t], h, 0, 0))

    tiles = [tile_spec(h, t) for h in range(nh) for t in range(n_tiles)]
    grid_spec = pltpu.PrefetchScalarGridSpec(
        num_scalar_prefetch=1,
        grid=(db,),
        in_specs=[col, col, col] + tiles + tiles,
        out_specs=col)
    return pl.pallas_call(
        functools.partial(_moba_step_kernel, n_tiles=n_tiles),
        grid_spec=grid_spec,
        out_shape=jax.ShapeDtypeStruct((db, w, 1), F32),
        compiler_params=_cparams(("arbitrary",)),
        name="moba_step",
    )(pages, q_col, kn_col, vn_col, *([cache_kt] * len(tiles)), *([cache_vt] * len(tiles)))


def _merge_kernel(x_ref, oa_ref, oc_ref, om_ref, gt_ref, wa_ref, wc_ref, wm_ref, wo_ref, g_ref, wr_ref, br_ref,
                  h_ref, hn_ref, te_ref, tw_ref):
    d = x_ref.shape[1]
    tm = x_ref.shape[0]
    merged = (gt_ref[:, 0:d] * _dot(oa_ref[...], wa_ref[...])
              + gt_ref[:, d:2 * d] * _dot(oc_ref[...], wc_ref[...])
              + gt_ref[:, 2 * d:3 * d] * _dot(om_ref[...], wm_ref[...]))
    h = x_ref[...] + _dot(merged.astype(BF16), wo_ref[...])
    h_ref[...] = h
    hn = _rms(h, g_ref[...])
    hi, lo = _split_bf16(hn)
    hn_ref[...] = hi
    whi, wlo = _split_bf16(wr_ref[...])
    logits = _dot(hi, whi) + _dot(lo, whi) + _dot(hi, wlo) + br_ref[...]
    ne = logits.shape[1]
    lane = lax.broadcasted_iota(I32, (tm, ne), 1)
    out_lane = lax.broadcasted_iota(I32, (tm, LANES), 1)
    s = logits
    vals, te = [], jnp.zeros((tm, LANES), I32)
    for t in range(TOP_K):
        mx = jnp.max(s, axis=-1, keepdims=True)
        idx = jnp.min(jnp.where(s == mx, lane, ne), axis=-1, keepdims=True)
        vals.append(mx)
        te = jnp.where(out_lane == t, idx, te)
        s = jnp.where(lane == idx, -jnp.inf, s)
    es = [jnp.exp(v - vals[0]) for v in vals]
    inv = 1.0 / functools.reduce(lambda a, b: a + b, es)
    tw = jnp.zeros((tm, LANES), F32)
    for t in range(TOP_K):
        tw = jnp.where(out_lane == t, es[t] * inv, tw)
    te_ref[...] = te
    tw_ref[...] = tw


def _merge(x2d, oa, oc, om, gates, wa, wc, wm, wo, g, wr, br, *, tm):
    r, d = x2d.shape
    ne = wr.shape[1]
    row = lambda width: pl.BlockSpec((tm, width), lambda i: (i, 0))
    return pl.pallas_call(
        _merge_kernel,
        grid=(r // tm,),
        in_specs=[row(d), row(oa.shape[1]), row(oc.shape[1]), row(om.shape[1]), row(gates.shape[1]),
                  _const_spec(wa.shape), _const_spec(wc.shape), _const_spec(wm.shape), _const_spec(wo.shape),
                  _const_spec((1, d)), _const_spec(wr.shape), _const_spec((1, ne))],
        out_specs=[row(d), row(d), row(LANES), row(LANES)],
        out_shape=[jax.ShapeDtypeStruct((r, d), F32), jax.ShapeDtypeStruct((r, d), BF16),
                   jax.ShapeDtypeStruct((r, LANES), I32), jax.ShapeDtypeStruct((r, LANES), F32)],
        compiler_params=_cparams(("arbitrary",)),
        name="merge",
    )(x2d, oa, oc, om, gates, wa, wc, wm, wo, g.reshape(1, d), wr, br.reshape(1, ne))


def _moe_kernel(be_ref, nu_ref, xs_ref, wgu_ref, bgu_ref, wd_ref, bd_ref, o_ref, wgu_b, wd_b):
    i = pl.program_id(0)
    e = be_ref[i]
    prev = be_ref[jnp.maximum(i - 1, 0)]

    @pl.when((i == 0) | (e != prev))
    def _():
        wgu_b[...] = wgu_ref[0].astype(BF16)
        wd_b[...] = wd_ref[0].astype(BF16)

    @pl.when(i < nu_ref[0])
    def _():
        f = wd_b.shape[0]
        gu = _dot(xs_ref[...], wgu_b[...]) + bgu_ref[0]
        g = jnp.minimum(gu[:, :f], SWIGLU_LIMIT)
        u = jnp.clip(gu[:, f:], -SWIGLU_LIMIT, SWIGLU_LIMIT)
        hdn = g * jax.nn.sigmoid(SWIGLU_ALPHA * g) * (u + 1.0)
        o_ref[...] = _dot(hdn.astype(BF16), wd_b[...]) + bd_ref[0]

    @pl.when(i >= nu_ref[0])
    def _():
        o_ref[...] = jnp.zeros(o_ref.shape, F32)


def _moe_experts(xs, blk_e, n_used, w_gate_up, b_gate_up, w_down, b_down):
    n_rows, d = xs.shape
    ne, _, f2 = w_gate_up.shape
    f = f2 // 2
    te = MOE_ROW_TILE
    grid_spec = pltpu.PrefetchScalarGridSpec(
        num_scalar_prefetch=2,
        grid=(n_rows // te,),
        in_specs=[pl.BlockSpec((te, d), lambda i, be, nu: (i, 0)),
                  pl.BlockSpec((1, d, f2), lambda i, be, nu: (be[i], 0, 0)),
                  pl.BlockSpec((1, 1, f2), lambda i, be, nu: (be[i], 0, 0)),
                  pl.BlockSpec((1, f, d), lambda i, be, nu: (be[i], 0, 0)),
                  pl.BlockSpec((1, 1, d), lambda i, be, nu: (be[i], 0, 0))],
        out_specs=pl.BlockSpec((te, d), lambda i, be, nu: (i, 0)),
        scratch_shapes=[pltpu.VMEM((d, f2), BF16), pltpu.VMEM((f, d), BF16)])
    return pl.pallas_call(
        _moe_kernel,
        grid_spec=grid_spec,
        out_shape=jax.ShapeDtypeStruct((n_rows, d), F32),
        compiler_params=_cparams(("arbitrary",)),
        name="moe_experts",
    )(blk_e, n_used, xs, w_gate_up, b_gate_up.reshape(ne, 1, f2), w_down, b_down.reshape(ne, 1, d))


def _moe_plan(top_e, n_tok, ne):
    ne_tile = MOE_ROW_TILE
    onehot = (top_e[:, :, None] == jnp.arange(ne, dtype=I32)[None, None, :]).astype(I32).sum(axis=1)
    incl = jnp.cumsum(onehot, axis=0)
    counts = incl[-1]
    rank = jnp.take_along_axis(incl - onehot, top_e, axis=1)
    padded = (counts + ne_tile - 1) // ne_tile * ne_tile
    pad_end = jnp.cumsum(padded)
    pad_start = pad_end - padded
    dest = pad_start[top_e] + rank
    n_tiles = (n_tok * TOP_K) // ne_tile + ne
    blk_e = jnp.minimum(jnp.searchsorted(pad_end, jnp.arange(n_tiles, dtype=I32) * ne_tile, side='right'),
                        ne - 1).astype(I32)
    n_used = (pad_end[-1] // ne_tile).astype(I32).reshape(1)
    return dest.astype(I32), blk_e, n_used, n_tiles * ne_tile


def _combine_kernel(h_ref, rows_ref, tw_ref, g_ref, y_ref):
    tw = tw_ref[...]
    h = h_ref[...]
    for t in range(rows_ref.shape[0]):
        h = h + tw[:, t:t + 1] * rows_ref[t]
    y_ref[...] = _rms(h, g_ref[...])


def _combine(h, rows, tw, g, *, tm):
    r, d = h.shape
    k = rows.shape[0]
    return pl.pallas_call(
        _combine_kernel,
        grid=(r // tm,),
        in_specs=[pl.BlockSpec((tm, d), lambda i: (i, 0)), pl.BlockSpec((k, tm, d), lambda i: (0, i, 0)),
                  pl.BlockSpec((tm, LANES), lambda i: (i, 0)), _const_spec((1, d))],
        out_specs=pl.BlockSpec((tm, d), lambda i: (i, 0)),
        out_shape=jax.ShapeDtypeStruct((r, d), F32),
        compiler_params=_cparams(("arbitrary",)),
        name="combine",
    )(h, rows, tw, g.reshape(1, d))


def kernel(x_prompt, x_sample, mem_prompt, cache_k, cache_v, page_table, state_conv, cache_mem_k, cache_mem_v,
           norm_mix_g, w_in, w_dw, b_dw, conv_ln_g, conv_ln_b, norm_mem_g, w_mem_kv, w_branch_a, w_branch_c,
           w_branch_m, w_out, norm_ffn_g, w_router, b_router, w_gate_up, b_gate_up, w_down, b_down, norm_final_g):
    bp, seq, d = x_prompt.shape
    db, dseq, _ = x_sample.shape
    assert bp == 1 and dseq == 1
    n_mem, m_heads, m_hd = cache_mem_k.shape[1:]
    mw = m_heads * m_hd
    m_scale = m_hd ** -0.5
    cw = w_dw.shape[1]
    kw = w_dw.shape[0]
    page = cache_k.shape[1]
    n_pages = page_table.shape[1]
    past = n_pages * page
    assert past % MOBA_BLOCK == 0 and cache_k.shape[2:] == (A_HEADS, A_HEAD_DIM)

    w_in_bf = w_in.astype(BF16)
    wa, wc, wm, wo = (w.astype(BF16) for w in (w_branch_a, w_branch_c, w_branch_m, w_out))

    xp = x_prompt.reshape(seq, d)
    kv_mem = _memory_kv(mem_prompt.reshape(n_mem, d), norm_mem_g, w_mem_kv)
    mk_p, mv_p = kv_mem[:, :mw], kv_mem[:, mw:]
    cos_p, sin_p = _rope_tables(jnp.arange(seq, dtype=I32))
    q_p, kf_p, vf_p, hg_p, qm_p, gt_p, kb_p, vt_p, ksum_p = _in_projection(
        xp, norm_mix_g, w_in_bf, cos_p, sin_p, tm=512, cw=cw, mw=mw, m_scale=m_scale, prompt=True)
    oa_p = _moba_prompt(q_p, kb_p, vt_p, ksum_p.reshape(seq // MOBA_BLOCK, A_WIDTH), qt=2)
    oc_p = _conv_prompt(hg_p, jnp.zeros((kw - 1, cw), F32), w_dw, b_dw, conv_ln_g, conv_ln_b,
                        tm=_row_tile(seq, 512))
    om_p = _memattn_prompt(qm_p, mk_p.astype(BF16), mv_p.astype(BF16), tm=_row_tile(seq, 512), hd=m_hd)
    h_p, hn_p, te_p, tw_p = _merge(xp, oa_p, oc_p, om_p, gt_p, wa, wc, wm, wo, norm_ffn_g, w_router, b_router,
                                   tm=_row_tile(seq, 256))

    xs = x_sample.reshape(db, d)
    cos_s, sin_s = _rope_tables(jnp.full((db,), past, I32))
    q_s, kf_s, vf_s, hg_s, qm_s, gt_s = _in_projection(
        xs, norm_mix_g, w_in_bf, cos_s, sin_s, tm=db, cw=cw, mw=mw, m_scale=m_scale, prompt=False)
    cache_kt = cache_k.transpose(0, 2, 3, 1)
    cache_vt = cache_v.transpose(0, 2, 3, 1)
    col = lambda a: a.astype(F32).reshape(db, A_WIDTH, 1)
    q_col = col(q_s)
    top = _block_select(cache_kt, page_table, q_col, pages_per_step=8)
    ppb = MOBA_BLOCK // page
    pages = page_table[jnp.arange(db)[:, None, None, None],
                       top[:, :, :, 0].transpose(0, 2, 1)[..., None] * ppb + jnp.arange(ppb)]
    oa_s = _moba_step(cache_kt, cache_vt, pages.reshape(-1), q_col, col(kf_s), col(vf_s))
    oa_s = oa_s.reshape(db, A_WIDTH).astype(BF16)
    oc_s = _conv_step(state_conv.transpose(1, 0, 2), hg_s, w_dw, b_dw, conv_ln_g, conv_ln_b)
    om_s = _memattn_step(qm_s, cache_mem_k.reshape(db, n_mem, mw), cache_mem_v.reshape(db, n_mem, mw), hd=m_hd)
    h_s, hn_s, te_s, tw_s = _merge(xs, oa_s, oc_s, om_s, gt_s, wa, wc, wm, wo, norm_ffn_g, w_router, b_router, tm=db)

    n_tok = seq + db
    hn = jnp.concatenate([hn_p, hn_s], axis=0)
    top_e = jnp.concatenate([te_p[:, :TOP_K], te_s[:, :TOP_K]], axis=0)
    tw = jnp.concatenate([tw_p, tw_s], axis=0)
    h_all = jnp.concatenate([h_p, h_s], axis=0)
    dest, blk_e, n_used, n_rows = _moe_plan(top_e, n_tok, w_router.shape[1])
    row_tok = jnp.zeros((n_rows,), I32).at[dest.reshape(-1)].set(
        jnp.repeat(jnp.arange(n_tok, dtype=I32), TOP_K))
    out_rows = _moe_experts(hn[row_tok], blk_e, n_used, w_gate_up, b_gate_up, w_down, b_down)
    y = _combine(h_all, out_rows[dest.T], tw, norm_final_g, tm=_row_tile(n_tok, 384))

    y_prompt = y[:seq].reshape(1, seq, d)
    y_sample = y[seq:].reshape(db, 1, d)
    k_prompt = kf_p.reshape(1, seq, A_HEADS, A_HEAD_DIM)
    v_prompt = vf_p.reshape(1, seq, A_HEADS, A_HEAD_DIM)
    conv_prompt = hg_p[seq - (kw - 1):].reshape(1, kw - 1, cw)
    mem_k_prompt = mk_p.reshape(1, n_mem, m_heads, m_hd)
    mem_v_prompt = mv_p.reshape(1, n_mem, m_heads, m_hd)
    k_sample = kf_s.reshape(db, 1, A_HEADS, A_HEAD_DIM)
    v_sample = vf_s.reshape(db, 1, A_HEADS, A_HEAD_DIM)
    conv_sample = jnp.concatenate([state_conv[:, 1:], hg_s[:, None, :]], axis=1)
    return (y_prompt, y_sample, k_prompt, v_prompt, conv_prompt, mem_k_prompt, mem_v_prompt,
            k_sample, v_sample, conv_sample)
```

```python
import functools
import math

import numpy as np
import jax
import jax.numpy as jnp
from jax import lax
from jax.experimental import pallas as pl
from jax.experimental.pallas import tpu as pltpu

F32 = jnp.float32
BF16 = jnp.bfloat16
I32 = jnp.int32

A_HEADS = 8
A_HEAD_DIM = 64
A_WIDTH = A_HEADS * A_HEAD_DIM
A_SCALE = A_HEAD_DIM ** -0.5
MOBA_BLOCK = 256
MOBA_TOPK = 3
ROPE_THETA = 10000.0
N_BRANCH = 3
TOP_K = 4
SWIGLU_LIMIT = 7.0
SWIGLU_ALPHA = 1.702
EPS = 1e-5
NEG = -1e30
LOG2E = math.log2(math.e)

LANES = 128
VMEM_LIMIT = 56 * 1024 * 1024
MOE_ROW_TILE = 256
V_AUG = A_HEAD_DIM + 16


def _cparams(sem):
    return pltpu.CompilerParams(dimension_semantics=sem, vmem_limit_bytes=VMEM_LIMIT)


def _row_tile(n, cap):
    t = min(cap, n) // 8 * 8
    while n % t:
        t -= 8
    return t


def _const_spec(shape):
    nd = len(shape)
    return pl.BlockSpec(shape, lambda *_: (0,) * nd, pipeline_mode=pl.Buffered(1))


def _rms(x, g):
    return x * lax.rsqrt(jnp.mean(x * x, axis=-1, keepdims=True) + EPS) * g


def _split_bf16(x):
    hi = x.astype(BF16)
    lo = (x - hi.astype(F32)).astype(BF16)
    return hi, lo


def _dot(a, b):
    return jnp.dot(a, b, preferred_element_type=F32)


def _dot_t(a, b):
    return lax.dot_general(a, b, (((1,), (1,)), ((), ())), preferred_element_type=F32)


def _memkv_kernel(mem_ref, g_ref, w_ref, kv_ref):
    xn = _rms(mem_ref[...], g_ref[...])
    kv_ref[...] = _dot(xn.astype(BF16), w_ref[...].astype(BF16))


def _memory_kv(mem2d, g, w):
    n, d = mem2d.shape
    return pl.pallas_call(
        _memkv_kernel,
        out_shape=jax.ShapeDtypeStruct((n, w.shape[1]), F32),
        compiler_params=_cparams(None),
        name="memory_kv",
    )(mem2d, g.reshape(1, d), w)


def _inproj_kernel(x_ref, g_ref, w_ref, cos_ref, sin_ref, q_ref, kf_ref, vf_ref, h_ref, qm_ref, gt_ref,
                   *prompt_refs, m_scale):
    tm = x_ref.shape[0]
    aw = q_ref.shape[1]
    cw = h_ref.shape[1]
    mw = qm_ref.shape[1]
    xb = _rms(x_ref[...], g_ref[...]).astype(BF16)

    def proj(c0, n):
        return _dot(xb, w_ref[:, c0:c0 + n])

    cos = cos_ref[...]
    sin = sin_ref[...]
    lane = lax.broadcasted_iota(I32, (tm, LANES), 1)
    first_half = (lane % A_HEAD_DIM) < (A_HEAD_DIM // 2)

    def rope(c):
        swapped = jnp.where(first_half, pltpu.roll(c, LANES - A_HEAD_DIM // 2, 1),
                            pltpu.roll(c, A_HEAD_DIM // 2, 1))
        return c * cos + swapped * sin

    q = proj(0, aw)
    k = proj(aw, aw)
    for j in range(aw // LANES):
        sl = slice(j * LANES, (j + 1) * LANES)
        q_ref[:, sl] = rope(q[:, sl]).astype(BF16)
        kf_ref[:, sl] = rope(k[:, sl])
    v = proj(2 * aw, aw)
    vf_ref[...] = v
    if prompt_refs:
        kb_ref, vt_ref, ks_ref = prompt_refs
        kb_ref[...] = (kf_ref[...] * (A_SCALE * LOG2E)).astype(BF16)
        for b in range(tm // MOBA_BLOCK):
            ks_ref[b] = jnp.sum(kf_ref[b * MOBA_BLOCK:(b + 1) * MOBA_BLOCK, :], axis=0, keepdims=True)
        vt = v.T
        ones = jnp.ones((V_AUG - A_HEAD_DIM, tm), BF16)
        for h in range(A_HEADS):
            vt_ref[h * V_AUG:h * V_AUG + A_HEAD_DIM, :] = vt[h * A_HEAD_DIM:(h + 1) * A_HEAD_DIM, :].astype(BF16)
            vt_ref[h * V_AUG + A_HEAD_DIM:(h + 1) * V_AUG, :] = ones
    a = proj(3 * aw, cw)
    gate = proj(3 * aw + cw, cw)
    h_ref[...] = a * jax.nn.sigmoid(gate)
    c0 = 3 * aw + 2 * cw
    qm_ref[...] = (proj(c0, mw) * m_scale).astype(BF16)
    c0 += mw
    gw = gt_ref.shape[1]
    step = 512
    for j in range(gw // step):
        gt_ref[:, j * step:(j + 1) * step] = jax.nn.sigmoid(proj(c0 + j * step, step))


def _in_projection(x2d, g, w_bf, cos, sin, *, tm, cw, mw, m_scale, prompt):
    r, d = x2d.shape
    aw = A_WIDTH
    gw = w_bf.shape[1] - (3 * aw + 2 * cw + mw)
    row = lambda width: pl.BlockSpec((tm, width), lambda i: (i, 0))
    out_shape = [jax.ShapeDtypeStruct((r, aw), BF16), jax.ShapeDtypeStruct((r, aw), F32),
                 jax.ShapeDtypeStruct((r, aw), F32), jax.ShapeDtypeStruct((r, cw), F32),
                 jax.ShapeDtypeStruct((r, mw), BF16), jax.ShapeDtypeStruct((r, gw), F32)]
    out_specs = [row(aw), row(aw), row(aw), row(cw), row(mw), row(gw)]
    if prompt:
        assert tm % MOBA_BLOCK == 0
        bpt = tm // MOBA_BLOCK
        out_shape += [jax.ShapeDtypeStruct((r, aw), BF16), jax.ShapeDtypeStruct((A_HEADS * V_AUG, r), BF16),
                      jax.ShapeDtypeStruct((r // MOBA_BLOCK, 1, aw), F32)]
        out_specs += [row(aw), pl.BlockSpec((A_HEADS * V_AUG, tm), lambda i: (0, i)),
                      pl.BlockSpec((bpt, 1, aw), lambda i: (i, 0, 0))]
    return pl.pallas_call(
        functools.partial(_inproj_kernel, m_scale=m_scale),
        grid=(r // tm,),
        in_specs=[row(d), _const_spec((1, d)), _const_spec(w_bf.shape), row(LANES), row(LANES)],
        out_specs=out_specs,
        out_shape=out_shape,
        compiler_params=_cparams(("arbitrary",)),
        name="in_projection",
    )(x2d, g.reshape(1, d), w_bf, cos, sin)


def _rope_tables(pos):
    half = A_HEAD_DIM // 2
    inv = ROPE_THETA ** (-jnp.arange(half, dtype=F32) / half)
    ang = pos.astype(F32)[:, None] * inv[None, :]
    cos = jnp.cos(ang)
    sin = jnp.sin(ang)
    reps = LANES // A_HEAD_DIM
    return (jnp.tile(jnp.concatenate([cos, cos], axis=-1), (1, reps)),
            jnp.tile(jnp.concatenate([-sin, sin], axis=-1), (1, reps)))


def _moba_kernel(qi_ref, ki_ref, q_ref, k_ref, vt_ref, ks_ref, o_ref, qaug, m_scr, acc, st_scr, *, qt):
    step = pl.program_id(0)
    qi = qi_ref[step]
    kb = ki_ref[step]
    dh = A_HEAD_DIM
    blk = MOBA_BLOCK
    n_chunks = A_WIDTH // LANES

    @pl.when(kb == 0)
    def _select():
        m_scr[...] = jnp.full(m_scr.shape, -jnp.inf, F32)
        acc[...] = jnp.zeros(acc.shape, F32)
        lane_k = lax.broadcasted_iota(I32, (dh, LANES), 1)
        rowid = lax.broadcasted_iota(I32, (dh, blk), 0)
        for s in range(qt):
            j = qi * qt + s
            qT = q_ref[s * blk:(s + 1) * blk, :].astype(F32).T
            for c in range(n_chunks):
                qTc = qT[c * LANES:(c + 1) * LANES, :].astype(BF16)
                ksc = ks_ref[:, c * LANES:(c + 1) * LANES]
                for par in range(2):
                    h = 2 * c + par
                    mine = (lane_k < dh) if par == 0 else (lane_k >= dh)
                    sc = _dot(jnp.where(mine, ksc, 0.0).astype(BF16), qTc)
                    sc = jnp.where(rowid < j, sc, NEG)
                    sel = rowid == j
                    for t in range(MOBA_TOPK):
                        mx = jnp.max(sc, axis=0, keepdims=True)
                        idx = jnp.min(jnp.where(sc == mx, rowid, dh), axis=0, keepdims=True)
                        pick = rowid == idx
                        sel = sel | (pick & (t < j))
                        sc = jnp.where(pick, -jnp.inf, sc)
                    bias = jnp.where(sel, 0.0, NEG).astype(BF16)
                    qh = qTc[par * dh:(par + 1) * dh, :]
                    qaug[s * A_HEADS + h, par * dh:(par + 1) * dh, :] = qh
                    qaug[s * A_HEADS + h, (1 - par) * dh:(2 - par) * dh, :] = bias

    def sweep(subs, masked):
        lane_k = lax.broadcasted_iota(I32, (blk, LANES), 1)
        keep = [jnp.where(lane_k < dh, 1.0, 0.0).astype(BF16), jnp.where(lane_k >= dh, 1.0, 0.0).astype(BF16)]
        onehot = [jnp.where(lane_k - dh == kb, 1.0, 0.0).astype(BF16), jnp.where(lane_k == kb, 1.0, 0.0).astype(BF16)]
        if masked:
            causal = lax.broadcasted_iota(I32, (blk, blk), 0) <= lax.broadcasted_iota(I32, (blk, blk), 1)
        chains = [(s, h) for s in subs for h in range(A_HEADS)]
        slots = st_scr.shape[0]

        def logits(i):
            s, h = chains[i]
            c, par = divmod(h, 2)
            ka = k_ref[:, c * LANES:(c + 1) * LANES] * keep[par] + onehot[par]
            st_scr[i % slots] = _dot(ka, qaug[s * A_HEADS + h])

        for i in range(min(slots - 1, len(chains))):
            logits(i)
        for i, (s, h) in enumerate(chains):
            if i + slots - 1 < len(chains):
                logits(i + slots - 1)
            a = s * A_HEADS + h
            st = st_scr[i % slots]
            if masked:
                st = jnp.where(causal, st, NEG)
            m_prev = m_scr[s, h:h + 1, :]
            m_new = jnp.maximum(m_prev, jnp.max(st, axis=0, keepdims=True))
            p = jnp.exp2(st - m_new).astype(BF16)
            acc[a] = jnp.exp2(m_prev - m_new) * acc[a] + _dot(vt_ref[h * V_AUG:(h + 1) * V_AUG, :], p)
            m_scr[s, h:h + 1, :] = m_new

    pl.when(kb < qi * qt)(functools.partial(sweep, list(range(qt)), False))
    for s in range(qt):
        j = qi * qt + s
        pl.when((kb >= qi * qt) & (kb < j))(functools.partial(sweep, [s], False))
        pl.when(kb == j)(functools.partial(sweep, [s], True))

    @pl.when(kb == qi * qt + qt - 1)
    def _finish():
        for s in range(qt):
            outs = []
            for h in range(A_HEADS):
                a = acc[s * A_HEADS + h]
                outs.append(a[0:dh, :] / a[dh:dh + 1, :])
            o_ref[s * blk:(s + 1) * blk, :] = jnp.concatenate(outs, axis=0).T.astype(BF16)


def _moba_prompt(q, kb, vt, ksum, *, qt):
    r, aw = q.shape
    nb = r // MOBA_BLOCK
    assert r % (qt * MOBA_BLOCK) == 0 and MOBA_TOPK <= nb <= A_HEAD_DIM and A_HEAD_DIM * 2 == LANES
    tq = qt * MOBA_BLOCK
    ks = jnp.zeros((A_HEAD_DIM, aw), F32).at[:nb].set(ksum)
    qi, ki = [], []
    for i in range(r // tq):
        for n in range((i + 1) * qt):
            qi.append(i)
            ki.append(n)
    grid_spec = pltpu.PrefetchScalarGridSpec(
        num_scalar_prefetch=2,
        grid=(len(qi),),
        in_specs=[pl.BlockSpec((tq, aw), lambda s, qi, ki: (qi[s], 0)),
                  pl.BlockSpec((MOBA_BLOCK, aw), lambda s, qi, ki: (ki[s], 0)),
                  pl.BlockSpec((A_HEADS * V_AUG, MOBA_BLOCK), lambda s, qi, ki: (0, ki[s])),
                  pl.BlockSpec(ks.shape, lambda s, qi, ki: (0, 0))],
        out_specs=pl.BlockSpec((tq, aw), lambda s, qi, ki: (qi[s], 0)),
        scratch_shapes=[pltpu.VMEM((qt * A_HEADS, LANES, MOBA_BLOCK), BF16),
                        pltpu.VMEM((qt, A_HEADS, MOBA_BLOCK), F32),
                        pltpu.VMEM((qt * A_HEADS, V_AUG, MOBA_BLOCK), F32),
                        pltpu.VMEM((9, MOBA_BLOCK, MOBA_BLOCK), F32)])
    return pl.pallas_call(
        functools.partial(_moba_kernel, qt=qt),
        grid_spec=grid_spec,
        out_shape=jax.ShapeDtypeStruct((r, aw), BF16),
        compiler_params=_cparams(("arbitrary",)),
        name="moba_prompt",
    )(jnp.asarray(np.array(qi, np.int32)), jnp.asarray(np.array(ki, np.int32)), q, kb, vt, ks)


def _ln_silu(y, g, b):
    mu = jnp.mean(y, axis=-1, keepdims=True)
    d = y - mu
    var = jnp.mean(d * d, axis=-1, keepdims=True)
    z = d * lax.rsqrt(var + EPS) * g + b
    return z * jax.nn.sigmoid(z)


def _conv_prompt_kernel(h_ref, st_ref, w_ref, b_ref, g_ref, beta_ref, o_ref, ext, *, halo, rc):
    tm = h_ref.shape[0]
    kw = w_ref.shape[0]

    @pl.when(pl.program_id(0) == 0)
    def _():
        ext[0:halo, :] = st_ref[...]

    ext[halo:halo + tm, :] = h_ref[...]
    off = halo - (kw - 1)
    for r0 in range(0, tm, rc):
        y = jnp.zeros((rc, h_ref.shape[1]), F32) + b_ref[...]
        for k in range(kw):
            y = y + w_ref[k:k + 1, :] * ext[r0 + off + k:r0 + off + k + rc, :]
        o_ref[r0:r0 + rc, :] = _ln_silu(y, g_ref[...], beta_ref[...]).astype(BF16)
    ext[0:halo, :] = ext[tm:tm + halo, :]


def _conv_prompt(h2d, state, w_dw, b_dw, ln_g, ln_b, *, tm):
    r, c = h2d.shape
    kw = w_dw.shape[0]
    halo = -(-(kw - 1) // 8) * 8
    st = jnp.zeros((halo, c), F32).at[halo - (kw - 1):].set(state)
    vec = lambda a: a.reshape(1, c)
    return pl.pallas_call(
        functools.partial(_conv_prompt_kernel, halo=halo, rc=64),
        grid=(r // tm,),
        in_specs=[pl.BlockSpec((tm, c), lambda i: (i, 0)), _const_spec((halo, c)), _const_spec((kw, c)),
                  _const_spec((1, c)), _const_spec((1, c)), _const_spec((1, c))],
        out_specs=pl.BlockSpec((tm, c), lambda i: (i, 0)),
        out_shape=jax.ShapeDtypeStruct((r, c), BF16),
        scratch_shapes=[pltpu.VMEM((tm + halo, c), F32)],
        compiler_params=_cparams(("arbitrary",)),
        name="conv_prompt",
    )(h2d, st, w_dw, vec(b_dw), vec(ln_g), vec(ln_b))


def _conv_step_kernel(st_ref, h_ref, w_ref, b_ref, g_ref, beta_ref, o_ref):
    kw = w_ref.shape[0]
    y = b_ref[...] + w_ref[kw - 1:kw, :] * h_ref[...]
    for k in range(kw - 1):
        y = y + w_ref[k:k + 1, :] * st_ref[k]
    o_ref[...] = _ln_silu(y, g_ref[...], beta_ref[...]).astype(BF16)


def _conv_step(state_t, h2d, w_dw, b_dw, ln_g, ln_b):
    b, c = h2d.shape
    vec = lambda a: a.reshape(1, c)
    return pl.pallas_call(
        _conv_step_kernel,
        out_shape=jax.ShapeDtypeStruct((b, c), BF16),
        compiler_params=_cparams(None),
        name="conv_step",
    )(state_t, h2d, w_dw, vec(b_dw), vec(ln_g), vec(ln_b))


def _memattn_kernel(q_ref, k_ref, v_ref, o_ref, *, hd):
    for h in range(q_ref.shape[1] // hd):
        sl = slice(h * hd, (h + 1) * hd)
        s = _dot_t(q_ref[:, sl], k_ref[:, sl])
        p = jnp.exp(s - jnp.max(s, axis=-1, keepdims=True))
        inv = 1.0 / jnp.sum(p, axis=-1, keepdims=True)
        o_ref[:, sl] = (_dot(p.astype(BF16), v_ref[:, sl]) * inv).astype(BF16)


def _memattn_prompt(qm, mk_bf, mv_bf, *, tm, hd):
    r, w = qm.shape
    return pl.pallas_call(
        functools.partial(_memattn_kernel, hd=hd),
        grid=(r // tm,),
        in_specs=[pl.BlockSpec((tm, w), lambda i: (i, 0)), _const_spec(mk_bf.shape), _const_spec(mv_bf.shape)],
        out_specs=pl.BlockSpec((tm, w), lambda i: (i, 0)),
        out_shape=jax.ShapeDtypeStruct((r, w), BF16),
        compiler_params=_cparams(("arbitrary",)),
        name="memattn_prompt",
    )(qm, mk_bf, mv_bf)


def _memattn_step_kernel(q_ref, k_ref, v_ref, o_ref, *, hd):
    w = q_ref.shape[2]
    head_of_lane = lax.broadcasted_iota(I32, (LANES, w), 1) // hd
    head_row = lax.broadcasted_iota(I32, (LANES, w), 0)
    own = head_of_lane == head_row
    expand = jnp.where(own, 1.0, 0.0).astype(BF16)
    q_heads = jnp.where(own, q_ref[0].astype(F32), 0.0).astype(BF16)
    s = _dot_t(k_ref[0].astype(BF16), q_heads)
    m = jnp.max(s, axis=0, keepdims=True)
    pe = _dot(jnp.exp(s - m).astype(BF16), expand)
    num = jnp.sum(pe * v_ref[0], axis=0, keepdims=True)
    den = jnp.sum(pe, axis=0, keepdims=True)
    o_ref[0] = (num / den).astype(BF16)


def _memattn_step(q, k, v, *, hd):
    b, w = q.shape
    nk = k.shape[1]
    assert w // hd <= LANES
    one = pl.BlockSpec((1, 1, w), lambda i: (i, 0, 0))
    many = pl.BlockSpec((1, nk, w), lambda i: (i, 0, 0))
    out = pl.pallas_call(
        functools.partial(_memattn_step_kernel, hd=hd),
        grid=(b,),
        in_specs=[one, many, many],
        out_specs=one,
        out_shape=jax.ShapeDtypeStruct((b, 1, w), BF16),
        compiler_params=_cparams(("arbitrary",)),
        name="memattn_step",
    )(q.reshape(b, 1, w), k, v)
    return out.reshape(b, w)


def _block_select_kernel(pt_ref, q_ref, *refs, pages_per_block, n_blocks):
    del pt_ref
    page_refs = refs[:-3]
    top_ref, qb, sc = refs[-3:]
    s = pl.program_id(1)
    nh, dh, page = page_refs[0].shape[1:]
    bps = len(page_refs) // pages_per_block
    sub = 8

    @pl.when(s == 0)
    def _():
        qb[...] = jnp.broadcast_to(q_ref[0], qb.shape)

    for m in range(bps):
        part = None
        for j in range(pages_per_block):
            x = page_refs[m * pages_per_block + j][0].reshape(nh * dh, page) * qb[...]
            x = jnp.sum(x.reshape(nh, dh // sub, sub, page), axis=1)
            part = x if part is None else part + x
        tot = jnp.sum(jnp.sum(part, axis=2, keepdims=True), axis=1)
        sc[s * bps + m] = tot * (1.0 / MOBA_BLOCK)

    @pl.when(s == pl.num_programs(1) - 1)
    def _():
        scores = sc[...]
        ids = lax.broadcasted_iota(I32, scores.shape, 0)
        for t in range(MOBA_TOPK):
            mx = jnp.max(scores, axis=0, keepdims=True)
            idx = jnp.min(jnp.where(scores == mx, ids, n_blocks), axis=0, keepdims=True)
            top_ref[0, t] = idx[0]
            scores = jnp.where(ids == idx, -jnp.inf, scores)


def _block_select(cache_kt, page_table, q_col, *, pages_per_step):
    db, n_pages = page_table.shape
    _, nh, dh, page = cache_kt.shape
    ppb = MOBA_BLOCK // page
    n_blocks = n_pages // ppb
    assert n_pages % pages_per_step == 0 and pages_per_step % ppb == 0 and n_blocks >= MOBA_TOPK

    def page_spec(j):
        return pl.BlockSpec((1, nh, dh, page), lambda b, s, pt: (pt[b, s * pages_per_step + j], 0, 0, 0))

    grid_spec = pltpu.PrefetchScalarGridSpec(
        num_scalar_prefetch=1,
        grid=(db, n_pages // pages_per_step),
        in_specs=[pl.BlockSpec((1, nh * dh, 1), lambda b, s, pt: (b, 0, 0))]
        + [page_spec(j) for j in range(pages_per_step)],
        out_specs=pl.BlockSpec((1, MOBA_TOPK, nh, 1), lambda b, s, pt: (b, 0, 0, 0)),
        scratch_shapes=[pltpu.VMEM((nh * dh, page), F32), pltpu.VMEM((n_blocks, nh, 1), F32)])
    return pl.pallas_call(
        functools.partial(_block_select_kernel, pages_per_block=ppb, n_blocks=n_blocks),
        grid_spec=grid_spec,
        out_shape=jax.ShapeDtypeStruct((db, MOBA_TOPK, nh, 1), I32),
        compiler_params=_cparams(("arbitrary", "arbitrary")),
        name="block_select",
    )(page_table, q_col, *([cache_kt] * pages_per_step))


def _moba_step_kernel(pg_ref, q_ref, kn_ref, vn_ref, *refs, n_tiles):
    del pg_ref
    o_ref = refs[-1]
    k_refs = refs[:A_HEADS * n_tiles]
    v_refs = refs[A_HEADS * n_tiles:2 * A_HEADS * n_tiles]
    dh = A_HEAD_DIM
    for h in range(A_HEADS):
        rows = slice(h * dh, (h + 1) * dh)
        q = q_ref[0, rows, :]
        s_new = jnp.sum(q * kn_ref[0, rows, :], axis=0, keepdims=True) * A_SCALE
        ss = [jnp.sum(k_refs[h * n_tiles + t][0, 0] * q, axis=0, keepdims=True) * A_SCALE
              for t in range(n_tiles)]
        m = s_new
        for s in ss:
            m = jnp.maximum(m, jnp.max(s, axis=-1, keepdims=True))
        p_new = jnp.exp(s_new - m)
        den = p_new
        num = p_new * vn_ref[0, rows, :]
        acc = None
        for t in range(n_tiles):
            p = jnp.exp(ss[t] - m)
            den = den + jnp.sum(p, axis=-1, keepdims=True)
            pv = v_refs[h * n_tiles + t][0, 0] * p
            acc = pv if acc is None else acc + pv
        num = num + jnp.sum(acc, axis=-1, keepdims=True)
        o_ref[0, rows, :] = num / den


def _moba_step(cache_kt, cache_vt, pages, q_col, kn_col, vn_col):
    db, w, _ = q_col.shape
    _, nh, dh, page = cache_kt.shape
    n_tiles = pages.shape[0] // (db * nh)
    col = pl.BlockSpec((1, w, 1), lambda b, pg: (b, 0, 0))

    def tile_spec(h, t):
        return pl.BlockSpec((1, 1, dh, page), lambda b, pg: (pg[(b * nh + h) * n_tiles + t], h, 0, 0))

    tiles = [tile_spec(h, t) for h in range(nh) for t in range(n_tiles)]
    grid_spec = pltpu.PrefetchScalarGridSpec(
        num_scalar_prefetch=1,
        grid=(db,),
        in_specs=[col, col, col] + tiles + tiles,
        out_specs=col)
    return pl.pallas_call(
        functools.partial(_moba_step_kernel, n_tiles=n_tiles),
        grid_spec=grid_spec,
        out_shape=jax.ShapeDtypeStruct((db, w, 1), F32),
        compiler_params=_cparams(("arbitrary",)),
        name="moba_step",
    )(pages, q_col, kn_col, vn_col, *([cache_kt] * len(tiles)), *([cache_vt] * len(tiles)))


def _merge_kernel(x_ref, oa_ref, oc_ref, om_ref, gt_ref, wa_ref, wc_ref, wm_ref, wo_ref, g_ref, wr_ref, br_ref,
                  cin_ref, h_ref, hn_ref, te_ref, tw_ref, rk_ref, cnt_ref, carry):
    d = x_ref.shape[1]
    tm = x_ref.shape[0]
    merged = (gt_ref[:, 0:d] * _dot(oa_ref[...], wa_ref[...])
              + gt_ref[:, d:2 * d] * _dot(oc_ref[...], wc_ref[...])
              + gt_ref[:, 2 * d:3 * d] * _dot(om_ref[...], wm_ref[...]))
    h = x_ref[...] + _dot(merged.astype(BF16), wo_ref[...])
    h_ref[...] = h
    hn = _rms(h, g_ref[...])
    hn_ref[...] = hn
    hi, lo = _split_bf16(hn)
    whi, wlo = _split_bf16(wr_ref[...])
    logits = _dot(hi, whi) + _dot(lo, whi) + _dot(hi, wlo) + br_ref[...]
    ne = logits.shape[1]
    lane = lax.broadcasted_iota(I32, (tm, ne), 1)
    out_lane = lax.broadcasted_iota(I32, (tm, LANES), 1)
    s = logits
    vals, picks, te = [], [], jnp.zeros((tm, LANES), I32)
    for t in range(TOP_K):
        mx = jnp.max(s, axis=-1, keepdims=True)
        idx = jnp.min(jnp.where(s == mx, lane, ne), axis=-1, keepdims=True)
        vals.append(mx)
        picks.append(lane == idx)
        te = jnp.where(out_lane == t, idx, te)
        s = jnp.where(picks[-1], -jnp.inf, s)
    es = [jnp.exp(v - vals[0]) for v in vals]
    inv = 1.0 / functools.reduce(lambda a, b: a + b, es)
    tw = jnp.zeros((tm, LANES), F32)
    for t in range(TOP_K):
        tw = jnp.where(out_lane == t, es[t] * inv, tw)
    te_ref[...] = te
    tw_ref[...] = tw

    @pl.when(pl.program_id(0) == 0)
    def _():
        carry[...] = cin_ref[...]

    chosen = functools.reduce(lambda a, b: a | b, picks)
    earlier = lax.broadcasted_iota(I32, (tm, tm), 1) < lax.broadcasted_iota(I32, (tm, tm), 0)
    before = carry[...] + _dot(jnp.where(earlier, 1.0, 0.0).astype(BF16), jnp.where(chosen, 1.0, 0.0).astype(BF16))
    rk = jnp.zeros((tm, LANES), I32)
    for t in range(TOP_K):
        r_t = jnp.sum(jnp.where(picks[t], before, 0.0), axis=-1, keepdims=True)
        rk = jnp.where(out_lane == t, r_t.astype(I32), rk)
    rk_ref[...] = rk
    carry[...] = carry[...] + jnp.sum(jnp.where(chosen, 1.0, 0.0), axis=0, keepdims=True)
    cnt_ref[...] = carry[...]


def _merge(x2d, oa, oc, om, gates, wa, wc, wm, wo, g, wr, br, counts_in, *, tm):
    r, d = x2d.shape
    ne = wr.shape[1]
    row = lambda width: pl.BlockSpec((tm, width), lambda i: (i, 0))
    return pl.pallas_call(
        _merge_kernel,
        grid=(r // tm,),
        in_specs=[row(d), row(oa.shape[1]), row(oc.shape[1]), row(om.shape[1]), row(gates.shape[1]),
                  _const_spec(wa.shape), _const_spec(wc.shape), _const_spec(wm.shape), _const_spec(wo.shape),
                  _const_spec((1, d)), _const_spec(wr.shape), _const_spec((1, ne)), _const_spec((1, ne))],
        out_specs=[row(d), row(d), row(LANES), row(LANES), row(LANES), pl.BlockSpec((1, ne), lambda i: (0, 0))],
        out_shape=[jax.ShapeDtypeStruct((r, d), F32), jax.ShapeDtypeStruct((r, d), F32),
                   jax.ShapeDtypeStruct((r, LANES), I32), jax.ShapeDtypeStruct((r, LANES), F32),
                   jax.ShapeDtypeStruct((r, LANES), I32), jax.ShapeDtypeStruct((1, ne), F32)],
        scratch_shapes=[pltpu.VMEM((1, ne), F32)],
        compiler_params=_cparams(("arbitrary",)),
        name="merge",
    )(x2d, oa, oc, om, gates, wa, wc, wm, wo, g.reshape(1, d), wr, br.reshape(1, ne), counts_in)


def _moe_kernel(be_ref, nu_ref, xs_ref, wgu_ref, bgu_ref, wd_ref, bd_ref, o_ref, wgu_b, wd_b):
    i = pl.program_id(0)
    e = be_ref[i]
    prev = be_ref[jnp.maximum(i - 1, 0)]

    @pl.when((i == 0) | (e != prev))
    def _():
        wgu_b[...] = wgu_ref[0].astype(BF16)
        wd_b[...] = wd_ref[0].astype(BF16)

    @pl.when(i < nu_ref[0])
    def _():
        f = wd_b.shape[0]
        gu = _dot(xs_ref[...].astype(BF16), wgu_b[...]) + bgu_ref[0]
        g = jnp.minimum(gu[:, :f], SWIGLU_LIMIT)
        u = jnp.clip(gu[:, f:], -SWIGLU_LIMIT, SWIGLU_LIMIT)
        hdn = g * jax.nn.sigmoid(SWIGLU_ALPHA * g) * (u + 1.0)
        o_ref[...] = _dot(hdn.astype(BF16), wd_b[...]) + bd_ref[0]

    @pl.when(i >= nu_ref[0])
    def _():
        o_ref[...] = jnp.zeros(o_ref.shape, F32)


def _moe_experts(xs, blk_e, n_used, w_gate_up, b_gate_up, w_down, b_down):
    n_rows, d = xs.shape
    ne, _, f2 = w_gate_up.shape
    f = f2 // 2
    te = MOE_ROW_TILE
    grid_spec = pltpu.PrefetchScalarGridSpec(
        num_scalar_prefetch=2,
        grid=(n_rows // te,),
        in_specs=[pl.BlockSpec((te, d), lambda i, be, nu: (i, 0)),
                  pl.BlockSpec((1, d, f2), lambda i, be, nu: (be[i], 0, 0)),
                  pl.BlockSpec((1, 1, f2), lambda i, be, nu: (be[i], 0, 0)),
                  pl.BlockSpec((1, f, d), lambda i, be, nu: (be[i], 0, 0)),
                  pl.BlockSpec((1, 1, d), lambda i, be, nu: (be[i], 0, 0))],
        out_specs=pl.BlockSpec((te, d), lambda i, be, nu: (i, 0)),
        scratch_shapes=[pltpu.VMEM((d, f2), BF16), pltpu.VMEM((f, d), BF16)])
    return pl.pallas_call(
        _moe_kernel,
        grid_spec=grid_spec,
        out_shape=jax.ShapeDtypeStruct((n_rows, d), F32),
        compiler_params=_cparams(("arbitrary",)),
        name="moe_experts",
    )(blk_e, n_used, xs, w_gate_up, b_gate_up.reshape(ne, 1, f2), w_down, b_down.reshape(ne, 1, d))


def _moe_plan(top_e, rank, counts, n_tok):
    ne_tile = MOE_ROW_TILE
    ne = counts.shape[0]
    padded = (counts + ne_tile - 1) // ne_tile * ne_tile
    pad_end = jnp.cumsum(padded)
    pad_start = pad_end - padded
    onehot = top_e[:, :, None] == jnp.arange(ne, dtype=I32)[None, None, :]
    dest = jnp.sum(jnp.where(onehot, pad_start[None, None, :], 0), axis=-1) + rank
    n_tiles = (n_tok * TOP_K) // ne_tile + ne
    tile_row = jnp.arange(n_tiles, dtype=I32) * ne_tile
    blk_e = jnp.minimum(jnp.sum((pad_end[None, :] <= tile_row[:, None]).astype(I32), axis=1), ne - 1)
    n_used = (pad_end[-1] // ne_tile).astype(I32).reshape(1)
    return dest.astype(I32), blk_e.astype(I32), n_used, n_tiles * ne_tile


def _combine_kernel(h_ref, rows_ref, tw_ref, g_ref, y_ref):
    tw = tw_ref[...]
    h = h_ref[...]
    for t in range(rows_ref.shape[0]):
        h = h + tw[:, t:t + 1] * rows_ref[t]
    y_ref[...] = _rms(h, g_ref[...])


def _combine(h, rows, tw, g, *, tm):
    r, d = h.shape
    k = rows.shape[0]
    return pl.pallas_call(
        _combine_kernel,
        grid=(r // tm,),
        in_specs=[pl.BlockSpec((tm, d), lambda i: (i, 0)), pl.BlockSpec((k, tm, d), lambda i: (0, i, 0)),
                  pl.BlockSpec((tm, LANES), lambda i: (i, 0)), _const_spec((1, d))],
        out_specs=pl.BlockSpec((tm, d), lambda i: (i, 0)),
        out_shape=jax.ShapeDtypeStruct((r, d), F32),
        compiler_params=_cparams(("arbitrary",)),
        name="combine",
    )(h, rows, tw, g.reshape(1, d))


def kernel(x_prompt, x_sample, mem_prompt, cache_k, cache_v, page_table, state_conv, cache_mem_k, cache_mem_v,
           norm_mix_g, w_in, w_dw, b_dw, conv_ln_g, conv_ln_b, norm_mem_g, w_mem_kv, w_branch_a, w_branch_c,
           w_branch_m, w_out, norm_ffn_g, w_router, b_router, w_gate_up, b_gate_up, w_down, b_down, norm_final_g):
    bp, seq, d = x_prompt.shape
    db, dseq, _ = x_sample.shape
    assert bp == 1 and dseq == 1
    n_mem, m_heads, m_hd = cache_mem_k.shape[1:]
    mw = m_heads * m_hd
    m_scale = m_hd ** -0.5
    cw = w_dw.shape[1]
    kw = w_dw.shape[0]
    page = cache_k.shape[1]
    n_pages = page_table.shape[1]
    past = n_pages * page
    assert past % MOBA_BLOCK == 0 and cache_k.shape[2:] == (A_HEADS, A_HEAD_DIM)

    w_in_bf = w_in.astype(BF16)
    wa, wc, wm, wo = (w.astype(BF16) for w in (w_branch_a, w_branch_c, w_branch_m, w_out))

    xp = x_prompt.reshape(seq, d)
    kv_mem = _memory_kv(mem_prompt.reshape(n_mem, d), norm_mem_g, w_mem_kv)
    mk_p, mv_p = kv_mem[:, :mw], kv_mem[:, mw:]
    cos_p, sin_p = _rope_tables(jnp.arange(seq, dtype=I32))
    q_p, kf_p, vf_p, hg_p, qm_p, gt_p, kb_p, vt_p, ksum_p = _in_projection(
        xp, norm_mix_g, w_in_bf, cos_p, sin_p, tm=512, cw=cw, mw=mw, m_scale=m_scale, prompt=True)
    oa_p = _moba_prompt(q_p, kb_p, vt_p, ksum_p.reshape(seq // MOBA_BLOCK, A_WIDTH), qt=4)
    oc_p = _conv_prompt(hg_p, jnp.zeros((kw - 1, cw), F32), w_dw, b_dw, conv_ln_g, conv_ln_b,
                        tm=_row_tile(seq, 512))
    om_p = _memattn_prompt(qm_p, mk_p.astype(BF16), mv_p.astype(BF16), tm=_row_tile(seq, 512), hd=m_hd)
    n_exp = w_router.shape[1]
    h_p, hn_p, te_p, tw_p, rk_p, cnt_p = _merge(xp, oa_p, oc_p, om_p, gt_p, wa, wc, wm, wo, norm_ffn_g, w_router,
                                                b_router, jnp.zeros((1, n_exp), F32), tm=_row_tile(seq, 256))

    xs = x_sample.reshape(db, d)
    cos_s, sin_s = _rope_tables(jnp.full((db,), past, I32))
    q_s, kf_s, vf_s, hg_s, qm_s, gt_s = _in_projection(
        xs, norm_mix_g, w_in_bf, cos_s, sin_s, tm=db, cw=cw, mw=mw, m_scale=m_scale, prompt=False)
    cache_kt = cache_k.transpose(0, 2, 3, 1)
    cache_vt = cache_v.transpose(0, 2, 3, 1)
    col = lambda a: a.astype(F32).reshape(db, A_WIDTH, 1)
    q_col = col(q_s)
    top = _block_select(cache_kt, page_table, q_col, pages_per_step=min(16, n_pages))
    ppb = MOBA_BLOCK // page
    pages = page_table[jnp.arange(db)[:, None, None, None],
                       top[:, :, :, 0].transpose(0, 2, 1)[..., None] * ppb + jnp.arange(ppb)]
    oa_s = _moba_step(cache_kt, cache_vt, pages.reshape(-1), q_col, col(kf_s), col(vf_s))
    oa_s = oa_s.reshape(db, A_WIDTH).astype(BF16)
    oc_s = _conv_step(state_conv.transpose(1, 0, 2), hg_s, w_dw, b_dw, conv_ln_g, conv_ln_b)
    om_s = _memattn_step(qm_s, cache_mem_k.reshape(db, n_mem, mw), cache_mem_v.reshape(db, n_mem, mw), hd=m_hd)
    h_s, hn_s, te_s, tw_s, rk_s, cnt = _merge(xs, oa_s, oc_s, om_s, gt_s, wa, wc, wm, wo, norm_ffn_g, w_router,
                                              b_router, cnt_p, tm=db)

    n_tok = seq + db
    hn = jnp.concatenate([hn_p, hn_s], axis=0)
    top_e = jnp.concatenate([te_p[:, :TOP_K], te_s[:, :TOP_K]], axis=0)
    rank = jnp.concatenate([rk_p[:, :TOP_K], rk_s[:, :TOP_K]], axis=0)
    tw = jnp.concatenate([tw_p, tw_s], axis=0)
    h_all = jnp.concatenate([h_p, h_s], axis=0)
    dest, blk_e, n_used, n_rows = _moe_plan(top_e, rank, cnt[0].astype(I32), n_tok)
    row_tok = jnp.zeros((n_rows,), I32).at[dest.reshape(-1)].set(
        jnp.repeat(jnp.arange(n_tok, dtype=I32), TOP_K))
    out_rows = _moe_experts(hn[row_tok], blk_e, n_used, w_gate_up, b_gate_up, w_down, b_down)
    y = _combine(h_all, out_rows[dest.T], tw, norm_final_g, tm=_row_tile(n_tok, 384))

    y_prompt = y[:seq].reshape(1, seq, d)
    y_sample = y[seq:].reshape(db, 1, d)
    k_prompt = kf_p.reshape(1, seq, A_HEADS, A_HEAD_DIM)
    v_prompt = vf_p.reshape(1, seq, A_HEADS, A_HEAD_DIM)
    conv_prompt = hg_p[seq - (kw - 1):].reshape(1, kw - 1, cw)
    mem_k_prompt = mk_p.reshape(1, n_mem, m_heads, m_hd)
    mem_v_prompt = mv_p.reshape(1, n_mem, m_heads, m_hd)
    k_sample = kf_s.reshape(db, 1, A_HEADS, A_HEAD_DIM)
    v_sample = vf_s.reshape(db, 1, A_HEADS, A_HEAD_DIM)
    conv_sample = jnp.concatenate([state_conv[:, 1:], hg_s[:, None, :]], axis=1)
    return (y_prompt, y_sample, k_prompt, v_prompt, conv_prompt, mem_k_prompt, mem_v_prompt,
            k_sample, v_sample, conv_sample)
```

```python
import functools
import math

import numpy as np
import jax
import jax.numpy as jnp
from jax import lax
from jax.experimental import pallas as pl
from jax.experimental.pallas import tpu as pltpu

F32 = jnp.float32
BF16 = jnp.bfloat16
I32 = jnp.int32

A_HEADS = 8
A_HEAD_DIM = 64
A_WIDTH = A_HEADS * A_HEAD_DIM
A_SCALE = A_HEAD_DIM ** -0.5
MOBA_BLOCK = 256
MOBA_TOPK = 3
ROPE_THETA = 10000.0
N_BRANCH = 3
TOP_K = 4
SWIGLU_LIMIT = 7.0
SWIGLU_ALPHA = 1.702
EPS = 1e-5
NEG = -1e30
LOG2E = math.log2(math.e)

LANES = 128
VMEM_LIMIT = 56 * 1024 * 1024
MOE_ROW_TILE = 256
V_AUG = A_HEAD_DIM + 16


def _cparams(sem):
    return pltpu.CompilerParams(dimension_semantics=sem, vmem_limit_bytes=VMEM_LIMIT)


def _row_tile(n, cap):
    t = min(cap, n) // 8 * 8
    while n % t:
        t -= 8
    return t


def _const_spec(shape):
    nd = len(shape)
    return pl.BlockSpec(shape, lambda *_: (0,) * nd, pipeline_mode=pl.Buffered(1))


def _rms(x, g):
    return x * lax.rsqrt(jnp.mean(x * x, axis=-1, keepdims=True) + EPS) * g


def _split_bf16(x):
    hi = x.astype(BF16)
    lo = (x - hi.astype(F32)).astype(BF16)
    return hi, lo


def _dot(a, b):
    return jnp.dot(a, b, preferred_element_type=F32)


def _dot_t(a, b):
    return lax.dot_general(a, b, (((1,), (1,)), ((), ())), preferred_element_type=F32)


def _memkv_kernel(mem_ref, g_ref, w_ref, kv_ref):
    xn = _rms(mem_ref[...], g_ref[...])
    kv_ref[...] = _dot(xn.astype(BF16), w_ref[...].astype(BF16))


def _memory_kv(mem2d, g, w):
    n, d = mem2d.shape
    return pl.pallas_call(
        _memkv_kernel,
        out_shape=jax.ShapeDtypeStruct((n, w.shape[1]), F32),
        compiler_params=_cparams(None),
        name="memory_kv",
    )(mem2d, g.reshape(1, d), w)


def _inproj_kernel(x_ref, g_ref, w_ref, cos_ref, sin_ref, q_ref, kf_ref, vf_ref, h_ref, qm_ref, gt_ref,
                   *prompt_refs, m_scale):
    tm = x_ref.shape[0]
    aw = q_ref.shape[1]
    cw = h_ref.shape[1]
    mw = qm_ref.shape[1]
    xb = _rms(x_ref[...], g_ref[...]).astype(BF16)

    def proj(c0, n):
        return _dot(xb, w_ref[:, c0:c0 + n])

    cos = cos_ref[...]
    sin = sin_ref[...]
    lane = lax.broadcasted_iota(I32, (tm, LANES), 1)
    first_half = (lane % A_HEAD_DIM) < (A_HEAD_DIM // 2)

    def rope(c):
        swapped = jnp.where(first_half, pltpu.roll(c, LANES - A_HEAD_DIM // 2, 1),
                            pltpu.roll(c, A_HEAD_DIM // 2, 1))
        return c * cos + swapped * sin

    q = proj(0, aw)
    k = proj(aw, aw)
    for j in range(aw // LANES):
        sl = slice(j * LANES, (j + 1) * LANES)
        q_ref[:, sl] = rope(q[:, sl]).astype(BF16)
        kf_ref[:, sl] = rope(k[:, sl])
    v = proj(2 * aw, aw)
    vf_ref[...] = v
    if prompt_refs:
        ka_ref, vt_ref, ks_ref = prompt_refs
        row_blk = (pl.program_id(0) * tm + lax.broadcasted_iota(I32, (tm, LANES), 0)) // MOBA_BLOCK
        for j in range(aw // LANES):
            kc = kf_ref[:, j * LANES:(j + 1) * LANES] * (A_SCALE * LOG2E)
            low = lane < A_HEAD_DIM
            even = jnp.where(low, kc, jnp.where(lane - A_HEAD_DIM == row_blk, 1.0, 0.0))
            odd = jnp.where(low, jnp.where(lane == row_blk, 1.0, 0.0), kc)
            ka_ref[:, (2 * j) * LANES:(2 * j + 1) * LANES] = even.astype(BF16)
            ka_ref[:, (2 * j + 1) * LANES:(2 * j + 2) * LANES] = odd.astype(BF16)
        for b in range(tm // MOBA_BLOCK):
            ks_ref[b] = jnp.sum(kf_ref[b * MOBA_BLOCK:(b + 1) * MOBA_BLOCK, :], axis=0, keepdims=True)
        vt = v.T
        ones = jnp.ones((V_AUG - A_HEAD_DIM, tm), BF16)
        for h in range(A_HEADS):
            vt_ref[h * V_AUG:h * V_AUG + A_HEAD_DIM, :] = vt[h * A_HEAD_DIM:(h + 1) * A_HEAD_DIM, :].astype(BF16)
            vt_ref[h * V_AUG + A_HEAD_DIM:(h + 1) * V_AUG, :] = ones
    a = proj(3 * aw, cw)
    gate = proj(3 * aw + cw, cw)
    h_ref[...] = a * jax.nn.sigmoid(gate)
    c0 = 3 * aw + 2 * cw
    qm_ref[...] = (proj(c0, mw) * m_scale).astype(BF16)
    c0 += mw
    gw = gt_ref.shape[1]
    step = 512
    for j in range(gw // step):
        gt_ref[:, j * step:(j + 1) * step] = jax.nn.sigmoid(proj(c0 + j * step, step))


def _in_projection(x2d, g, w_bf, cos, sin, *, tm, cw, mw, m_scale, prompt):
    r, d = x2d.shape
    aw = A_WIDTH
    gw = w_bf.shape[1] - (3 * aw + 2 * cw + mw)
    row = lambda width: pl.BlockSpec((tm, width), lambda i: (i, 0))
    out_shape = [jax.ShapeDtypeStruct((r, aw), BF16), jax.ShapeDtypeStruct((r, aw), F32),
                 jax.ShapeDtypeStruct((r, aw), F32), jax.ShapeDtypeStruct((r, cw), F32),
                 jax.ShapeDtypeStruct((r, mw), BF16), jax.ShapeDtypeStruct((r, gw), F32)]
    out_specs = [row(aw), row(aw), row(aw), row(cw), row(mw), row(gw)]
    if prompt:
        assert tm % MOBA_BLOCK == 0
        bpt = tm // MOBA_BLOCK
        out_shape += [jax.ShapeDtypeStruct((r, A_HEADS * LANES), BF16),
                      jax.ShapeDtypeStruct((A_HEADS * V_AUG, r), BF16),
                      jax.ShapeDtypeStruct((r // MOBA_BLOCK, 1, aw), F32)]
        out_specs += [row(A_HEADS * LANES), pl.BlockSpec((A_HEADS * V_AUG, tm), lambda i: (0, i)),
                      pl.BlockSpec((bpt, 1, aw), lambda i: (i, 0, 0))]
    return pl.pallas_call(
        functools.partial(_inproj_kernel, m_scale=m_scale),
        grid=(r // tm,),
        in_specs=[row(d), _const_spec((1, d)), _const_spec(w_bf.shape), row(LANES), row(LANES)],
        out_specs=out_specs,
        out_shape=out_shape,
        compiler_params=_cparams(("arbitrary",)),
        name="in_projection",
    )(x2d, g.reshape(1, d), w_bf, cos, sin)


def _rope_tables(pos):
    half = A_HEAD_DIM // 2
    inv = ROPE_THETA ** (-jnp.arange(half, dtype=F32) / half)
    ang = pos.astype(F32)[:, None] * inv[None, :]
    cos = jnp.cos(ang)
    sin = jnp.sin(ang)
    reps = LANES // A_HEAD_DIM
    return (jnp.tile(jnp.concatenate([cos, cos], axis=-1), (1, reps)),
            jnp.tile(jnp.concatenate([-sin, sin], axis=-1), (1, reps)))


def _moba_kernel(qi_ref, ki_ref, q_ref, k_ref, vt_ref, ks_ref, o_ref, qaug, m_scr, acc, st_scr, *, qt):
    step = pl.program_id(0)
    qi = qi_ref[step]
    kb = ki_ref[step]
    dh = A_HEAD_DIM
    blk = MOBA_BLOCK
    n_chunks = A_WIDTH // LANES

    @pl.when(kb == 0)
    def _select():
        m_scr[...] = jnp.full(m_scr.shape, -jnp.inf, F32)
        acc[...] = jnp.zeros(acc.shape, F32)
        lane_k = lax.broadcasted_iota(I32, (dh, LANES), 1)
        rowid = lax.broadcasted_iota(I32, (dh, blk), 0)
        for s in range(qt):
            j = qi * qt + s
            qT = q_ref[s * blk:(s + 1) * blk, :].astype(F32).T
            for c in range(n_chunks):
                qTc = qT[c * LANES:(c + 1) * LANES, :].astype(BF16)
                ksc = ks_ref[:, c * LANES:(c + 1) * LANES]
                for par in range(2):
                    h = 2 * c + par
                    mine = (lane_k < dh) if par == 0 else (lane_k >= dh)
                    sc = _dot(jnp.where(mine, ksc, 0.0).astype(BF16), qTc)
                    sc = jnp.where(rowid < j, sc, NEG)
                    sel = rowid == j
                    for t in range(MOBA_TOPK):
                        mx = jnp.max(sc, axis=0, keepdims=True)
                        idx = jnp.min(jnp.where(sc == mx, rowid, dh), axis=0, keepdims=True)
                        pick = rowid == idx
                        sel = sel | (pick & (t < j))
                        sc = jnp.where(pick, -jnp.inf, sc)
                    bias = jnp.where(sel, 0.0, NEG).astype(BF16)
                    qh = qTc[par * dh:(par + 1) * dh, :]
                    qaug[s * A_HEADS + h, par * dh:(par + 1) * dh, :] = qh
                    qaug[s * A_HEADS + h, (1 - par) * dh:(2 - par) * dh, :] = bias

    def sweep(subs, masked):
        if masked:
            causal = lax.broadcasted_iota(I32, (blk, blk), 0) <= lax.broadcasted_iota(I32, (blk, blk), 1)
        chains = [(s, h) for s in subs for h in range(A_HEADS)]
        slots = st_scr.shape[0]

        def logits(i):
            s, h = chains[i]
            st_scr[i % slots] = _dot(k_ref[:, h * LANES:(h + 1) * LANES], qaug[s * A_HEADS + h])

        for i in range(min(slots - 1, len(chains))):
            logits(i)
        for i, (s, h) in enumerate(chains):
            if i + slots - 1 < len(chains):
                logits(i + slots - 1)
            a = s * A_HEADS + h
            st = st_scr[i % slots]
            if masked:
                st = jnp.where(causal, st, NEG)
            m_prev = m_scr[s, h:h + 1, :]
            m_new = jnp.maximum(m_prev, jnp.max(st, axis=0, keepdims=True))
            p = jnp.exp2(st - m_new).astype(BF16)
            acc[a] = jnp.exp2(m_prev - m_new) * acc[a] + _dot(vt_ref[h * V_AUG:(h + 1) * V_AUG, :], p)
            m_scr[s, h:h + 1, :] = m_new

    pl.when(kb < qi * qt)(functools.partial(sweep, list(range(qt)), False))
    for s in range(qt):
        j = qi * qt + s
        pl.when((kb >= qi * qt) & (kb < j))(functools.partial(sweep, [s], False))
        pl.when(kb == j)(functools.partial(sweep, [s], True))

    @pl.when(kb == qi * qt + qt - 1)
    def _finish():
        for s in range(qt):
            outs = []
            for h in range(A_HEADS):
                a = acc[s * A_HEADS + h]
                outs.append(a[0:dh, :] / a[dh:dh + 1, :])
            o_ref[s * blk:(s + 1) * blk, :] = jnp.concatenate(outs, axis=0).T.astype(BF16)


def _moba_prompt(q, ka, vt, ksum, *, qt):
    r, aw = q.shape
    nb = r // MOBA_BLOCK
    assert r % (qt * MOBA_BLOCK) == 0 and MOBA_TOPK <= nb <= A_HEAD_DIM and A_HEAD_DIM * 2 == LANES
    tq = qt * MOBA_BLOCK
    ks = jnp.zeros((A_HEAD_DIM, aw), F32).at[:nb].set(ksum)
    qi, ki = [], []
    for i in range(r // tq):
        for n in range((i + 1) * qt):
            qi.append(i)
            ki.append(n)
    grid_spec = pltpu.PrefetchScalarGridSpec(
        num_scalar_prefetch=2,
        grid=(len(qi),),
        in_specs=[pl.BlockSpec((tq, aw), lambda s, qi, ki: (qi[s], 0)),
                  pl.BlockSpec((MOBA_BLOCK, A_HEADS * LANES), lambda s, qi, ki: (ki[s], 0)),
                  pl.BlockSpec((A_HEADS * V_AUG, MOBA_BLOCK), lambda s, qi, ki: (0, ki[s])),
                  pl.BlockSpec(ks.shape, lambda s, qi, ki: (0, 0))],
        out_specs=pl.BlockSpec((tq, aw), lambda s, qi, ki: (qi[s], 0)),
        scratch_shapes=[pltpu.VMEM((qt * A_HEADS, LANES, MOBA_BLOCK), BF16),
                        pltpu.VMEM((qt, A_HEADS, MOBA_BLOCK), F32),
                        pltpu.VMEM((qt * A_HEADS, V_AUG, MOBA_BLOCK), F32),
                        pltpu.VMEM((9, MOBA_BLOCK, MOBA_BLOCK), F32)])
    return pl.pallas_call(
        functools.partial(_moba_kernel, qt=qt),
        grid_spec=grid_spec,
        out_shape=jax.ShapeDtypeStruct((r, aw), BF16),
        compiler_params=_cparams(("arbitrary",)),
        name="moba_prompt",
    )(jnp.asarray(np.array(qi, np.int32)), jnp.asarray(np.array(ki, np.int32)), q, ka, vt, ks)


def _ln_silu(y, g, b):
    mu = jnp.mean(y, axis=-1, keepdims=True)
    d = y - mu
    var = jnp.mean(d * d, axis=-1, keepdims=True)
    z = d * lax.rsqrt(var + EPS) * g + b
    return z * jax.nn.sigmoid(z)


def _conv_prompt_kernel(h_ref, st_ref, w_ref, b_ref, g_ref, beta_ref, o_ref, ext, *, halo, rc):
    tm = h_ref.shape[0]
    kw = w_ref.shape[0]

    @pl.when(pl.program_id(0) == 0)
    def _():
        ext[0:halo, :] = st_ref[...]

    ext[halo:halo + tm, :] = h_ref[...]
    off = halo - (kw - 1)
    for r0 in range(0, tm, rc):
        y = jnp.zeros((rc, h_ref.shape[1]), F32) + b_ref[...]
        for k in range(kw):
            y = y + w_ref[k:k + 1, :] * ext[r0 + off + k:r0 + off + k + rc, :]
        o_ref[r0:r0 + rc, :] = _ln_silu(y, g_ref[...], beta_ref[...]).astype(BF16)
    ext[0:halo, :] = ext[tm:tm + halo, :]


def _conv_prompt(h2d, state, w_dw, b_dw, ln_g, ln_b, *, tm):
    r, c = h2d.shape
    kw = w_dw.shape[0]
    halo = -(-(kw - 1) // 8) * 8
    st = jnp.zeros((halo, c), F32).at[halo - (kw - 1):].set(state)
    vec = lambda a: a.reshape(1, c)
    return pl.pallas_call(
        functools.partial(_conv_prompt_kernel, halo=halo, rc=64),
        grid=(r // tm,),
        in_specs=[pl.BlockSpec((tm, c), lambda i: (i, 0)), _const_spec((halo, c)), _const_spec((kw, c)),
                  _const_spec((1, c)), _const_spec((1, c)), _const_spec((1, c))],
        out_specs=pl.BlockSpec((tm, c), lambda i: (i, 0)),
        out_shape=jax.ShapeDtypeStruct((r, c), BF16),
        scratch_shapes=[pltpu.VMEM((tm + halo, c), F32)],
        compiler_params=_cparams(("arbitrary",)),
        name="conv_prompt",
    )(h2d, st, w_dw, vec(b_dw), vec(ln_g), vec(ln_b))


def _conv_step_kernel(st_ref, h_ref, w_ref, b_ref, g_ref, beta_ref, o_ref):
    kw = w_ref.shape[0]
    y = b_ref[...] + w_ref[kw - 1:kw, :] * h_ref[...]
    for k in range(kw - 1):
        y = y + w_ref[k:k + 1, :] * st_ref[k]
    o_ref[...] = _ln_silu(y, g_ref[...], beta_ref[...]).astype(BF16)


def _conv_step(state_t, h2d, w_dw, b_dw, ln_g, ln_b):
    b, c = h2d.shape
    vec = lambda a: a.reshape(1, c)
    return pl.pallas_call(
        _conv_step_kernel,
        out_shape=jax.ShapeDtypeStruct((b, c), BF16),
        compiler_params=_cparams(None),
        name="conv_step",
    )(state_t, h2d, w_dw, vec(b_dw), vec(ln_g), vec(ln_b))


def _memattn_kernel(q_ref, k_ref, v_ref, o_ref, *, hd):
    for h in range(q_ref.shape[1] // hd):
        sl = slice(h * hd, (h + 1) * hd)
        s = _dot_t(q_ref[:, sl], k_ref[:, sl])
        p = jnp.exp(s - jnp.max(s, axis=-1, keepdims=True))
        inv = 1.0 / jnp.sum(p, axis=-1, keepdims=True)
        o_ref[:, sl] = (_dot(p.astype(BF16), v_ref[:, sl]) * inv).astype(BF16)


def _memattn_prompt(qm, mk_bf, mv_bf, *, tm, hd):
    r, w = qm.shape
    return pl.pallas_call(
        functools.partial(_memattn_kernel, hd=hd),
        grid=(r // tm,),
        in_specs=[pl.BlockSpec((tm, w), lambda i: (i, 0)), _const_spec(mk_bf.shape), _const_spec(mv_bf.shape)],
        out_specs=pl.BlockSpec((tm, w), lambda i: (i, 0)),
        out_shape=jax.ShapeDtypeStruct((r, w), BF16),
        compiler_params=_cparams(("arbitrary",)),
        name="memattn_prompt",
    )(qm, mk_bf, mv_bf)


def _memattn_step_kernel(q_ref, k_ref, v_ref, o_ref, *, hd):
    w = q_ref.shape[2]
    head_of_lane = lax.broadcasted_iota(I32, (LANES, w), 1) // hd
    head_row = lax.broadcasted_iota(I32, (LANES, w), 0)
    own = head_of_lane == head_row
    expand = jnp.where(own, 1.0, 0.0).astype(BF16)
    q_heads = jnp.where(own, q_ref[0].astype(F32), 0.0).astype(BF16)
    s = _dot_t(k_ref[0].astype(BF16), q_heads)
    m = jnp.max(s, axis=0, keepdims=True)
    pe = _dot(jnp.exp(s - m).astype(BF16), expand)
    num = jnp.sum(pe * v_ref[0], axis=0, keepdims=True)
    den = jnp.sum(pe, axis=0, keepdims=True)
    o_ref[0] = (num / den).astype(BF16)


def _memattn_step(q, k, v, *, hd):
    b, w = q.shape
    nk = k.shape[1]
    assert w // hd <= LANES
    one = pl.BlockSpec((1, 1, w), lambda i: (i, 0, 0))
    many = pl.BlockSpec((1, nk, w), lambda i: (i, 0, 0))
    out = pl.pallas_call(
        functools.partial(_memattn_step_kernel, hd=hd),
        grid=(b,),
        in_specs=[one, many, many],
        out_specs=one,
        out_shape=jax.ShapeDtypeStruct((b, 1, w), BF16),
        compiler_params=_cparams(("arbitrary",)),
        name="memattn_step",
    )(q.reshape(b, 1, w), k, v)
    return out.reshape(b, w)


def _block_select_kernel(pt_ref, q_ref, *refs, pages_per_block, n_blocks):
    del pt_ref
    page_refs = refs[:-3]
    top_ref, qb, sc = refs[-3:]
    s = pl.program_id(1)
    nh, dh, page = page_refs[0].shape[1:]
    bps = len(page_refs) // pages_per_block
    sub = 8

    @pl.when(s == 0)
    def _():
        qb[...] = jnp.broadcast_to(q_ref[0], qb.shape)

    for m in range(bps):
        part = None
        for j in range(pages_per_block):
            x = page_refs[m * pages_per_block + j][0].reshape(nh * dh, page) * qb[...]
            x = jnp.sum(x.reshape(nh, dh // sub, sub, page), axis=1)
            part = x if part is None else part + x
        tot = jnp.sum(jnp.sum(part, axis=2, keepdims=True), axis=1)
        sc[s * bps + m] = tot * (1.0 / MOBA_BLOCK)

    @pl.when(s == pl.num_programs(1) - 1)
    def _():
        scores = sc[...]
        ids = lax.broadcasted_iota(I32, scores.shape, 0)
        for t in range(MOBA_TOPK):
            mx = jnp.max(scores, axis=0, keepdims=True)
            idx = jnp.min(jnp.where(scores == mx, ids, n_blocks), axis=0, keepdims=True)
            top_ref[0, t] = idx[0]
            scores = jnp.where(ids == idx, -jnp.inf, scores)


def _block_select(cache_kt, page_table, q_col, *, pages_per_step):
    db, n_pages = page_table.shape
    _, nh, dh, page = cache_kt.shape
    ppb = MOBA_BLOCK // page
    n_blocks = n_pages // ppb
    assert n_pages % pages_per_step == 0 and pages_per_step % ppb == 0 and n_blocks >= MOBA_TOPK

    def page_spec(j):
        return pl.BlockSpec((1, nh, dh, page), lambda b, s, pt: (pt[b, s * pages_per_step + j], 0, 0, 0))

    grid_spec = pltpu.PrefetchScalarGridSpec(
        num_scalar_prefetch=1,
        grid=(db, n_pages // pages_per_step),
        in_specs=[pl.BlockSpec((1, nh * dh, 1), lambda b, s, pt: (b, 0, 0))]
        + [page_spec(j) for j in range(pages_per_step)],
        out_specs=pl.BlockSpec((1, MOBA_TOPK, nh, 1), lambda b, s, pt: (b, 0, 0, 0)),
        scratch_shapes=[pltpu.VMEM((nh * dh, page), F32), pltpu.VMEM((n_blocks, nh, 1), F32)])
    return pl.pallas_call(
        functools.partial(_block_select_kernel, pages_per_block=ppb, n_blocks=n_blocks),
        grid_spec=grid_spec,
        out_shape=jax.ShapeDtypeStruct((db, MOBA_TOPK, nh, 1), I32),
        compiler_params=_cparams(("arbitrary", "arbitrary")),
        name="block_select",
    )(page_table, q_col, *([cache_kt] * pages_per_step))


def _moba_step_kernel(pg_ref, q_ref, kn_ref, vn_ref, kt_hbm, vt_hbm, o_ref, kbuf, vbuf, sem, *, n_tiles):
    b = pl.program_id(0)
    dh = A_HEAD_DIM

    def tile_copies(seq, slot):
        out = []
        for h in range(A_HEADS):
            for t in range(n_tiles):
                pg = pg_ref[(seq * A_HEADS + h) * n_tiles + t]
                out.append(pltpu.make_async_copy(kt_hbm.at[pg, h], kbuf.at[slot, h, t], sem.at[slot, 0]))
                out.append(pltpu.make_async_copy(vt_hbm.at[pg, h], vbuf.at[slot, h, t], sem.at[slot, 1]))
        return out

    @pl.when(b == 0)
    def _():
        for cp in tile_copies(0, 0):
            cp.start()

    @pl.when(b + 1 < pl.num_programs(0))
    def _():
        for cp in tile_copies(b + 1, (b + 1) % 2):
            cp.start()

    slot = b % 2
    for cp in tile_copies(b, slot):
        cp.wait()

    for h in range(A_HEADS):
        rows = slice(h * dh, (h + 1) * dh)
        q = q_ref[0, rows, :]
        s_new = jnp.sum(q * kn_ref[0, rows, :], axis=0, keepdims=True) * A_SCALE
        ss = [jnp.sum(kbuf[slot, h, t] * q, axis=0, keepdims=True) * A_SCALE
              for t in range(n_tiles)]
        m = s_new
        for s in ss:
            m = jnp.maximum(m, jnp.max(s, axis=-1, keepdims=True))
        p_new = jnp.exp(s_new - m)
        den = p_new
        num = p_new * vn_ref[0, rows, :]
        acc = None
        for t in range(n_tiles):
            p = jnp.exp(ss[t] - m)
            den = den + jnp.sum(p, axis=-1, keepdims=True)
            pv = vbuf[slot, h, t] * p
            acc = pv if acc is None else acc + pv
        num = num + jnp.sum(acc, axis=-1, keepdims=True)
        o_ref[0, rows, :] = num / den


def _moba_step(cache_kt, cache_vt, pages, q_col, kn_col, vn_col):
    db, w, _ = q_col.shape
    _, nh, dh, page = cache_kt.shape
    n_tiles = pages.shape[0] // (db * nh)
    col = pl.BlockSpec((1, w, 1), lambda b, pg: (b, 0, 0))
    hbm = pl.BlockSpec(memory_space=pl.ANY)
    grid_spec = pltpu.PrefetchScalarGridSpec(
        num_scalar_prefetch=1,
        grid=(db,),
        in_specs=[col, col, col, hbm, hbm],
        out_specs=col,
        scratch_shapes=[pltpu.VMEM((2, nh, n_tiles, dh, page), F32), pltpu.VMEM((2, nh, n_tiles, dh, page), F32),
                        pltpu.SemaphoreType.DMA((2, 2))])
    return pl.pallas_call(
        functools.partial(_moba_step_kernel, n_tiles=n_tiles),
        grid_spec=grid_spec,
        out_shape=jax.ShapeDtypeStruct((db, w, 1), F32),
        compiler_params=_cparams(("arbitrary",)),
        name="moba_step",
    )(pages, q_col, kn_col, vn_col, cache_kt, cache_vt)


def _merge_kernel(x_ref, oa_ref, oc_ref, om_ref, gt_ref, wa_ref, wc_ref, wm_ref, wo_ref, g_ref, wr_ref, br_ref,
                  cin_ref, h_ref, hn_ref, te_ref, tw_ref, rk_ref, cnt_ref, carry):
    d = x_ref.shape[1]
    tm = x_ref.shape[0]
    merged = (gt_ref[:, 0:d] * _dot(oa_ref[...], wa_ref[...])
              + gt_ref[:, d:2 * d] * _dot(oc_ref[...], wc_ref[...])
              + gt_ref[:, 2 * d:3 * d] * _dot(om_ref[...], wm_ref[...]))
    h = x_ref[...] + _dot(merged.astype(BF16), wo_ref[...])
    h_ref[...] = h
    hn = _rms(h, g_ref[...])
    hn_ref[...] = hn
    hi, lo = _split_bf16(hn)
    whi, wlo = _split_bf16(wr_ref[...])
    logits = _dot(hi, whi) + _dot(lo, whi) + _dot(hi, wlo) + br_ref[...]
    ne = logits.shape[1]
    lane = lax.broadcasted_iota(I32, (tm, ne), 1)
    out_lane = lax.broadcasted_iota(I32, (tm, LANES), 1)
    s = logits
    vals, picks, te = [], [], jnp.zeros((tm, LANES), I32)
    for t in range(TOP_K):
        mx = jnp.max(s, axis=-1, keepdims=True)
        idx = jnp.min(jnp.where(s == mx, lane, ne), axis=-1, keepdims=True)
        vals.append(mx)
        picks.append(lane == idx)
        te = jnp.where(out_lane == t, idx, te)
        s = jnp.where(picks[-1], -jnp.inf, s)
    es = [jnp.exp(v - vals[0]) for v in vals]
    inv = 1.0 / functools.reduce(lambda a, b: a + b, es)
    tw = jnp.zeros((tm, LANES), F32)
    for t in range(TOP_K):
        tw = jnp.where(out_lane == t, es[t] * inv, tw)
    te_ref[...] = te
    tw_ref[...] = tw

    @pl.when(pl.program_id(0) == 0)
    def _():
        carry[...] = cin_ref[...]

    chosen = functools.reduce(lambda a, b: a | b, picks)
    earlier = lax.broadcasted_iota(I32, (tm, tm), 1) < lax.broadcasted_iota(I32, (tm, tm), 0)
    before = carry[...] + _dot(jnp.where(earlier, 1.0, 0.0).astype(BF16), jnp.where(chosen, 1.0, 0.0).astype(BF16))
    rk = jnp.zeros((tm, LANES), I32)
    for t in range(TOP_K):
        r_t = jnp.sum(jnp.where(picks[t], before, 0.0), axis=-1, keepdims=True)
        rk = jnp.where(out_lane == t, r_t.astype(I32), rk)
    rk_ref[...] = rk
    carry[...] = carry[...] + jnp.sum(jnp.where(chosen, 1.0, 0.0), axis=0, keepdims=True)
    cnt_ref[...] = carry[...]


def _merge(x2d, oa, oc, om, gates, wa, wc, wm, wo, g, wr, br, counts_in, *, tm):
    r, d = x2d.shape
    ne = wr.shape[1]
    row = lambda width: pl.BlockSpec((tm, width), lambda i: (i, 0))
    return pl.pallas_call(
        _merge_kernel,
        grid=(r // tm,),
        in_specs=[row(d), row(oa.shape[1]), row(oc.shape[1]), row(om.shape[1]), row(gates.shape[1]),
                  _const_spec(wa.shape), _const_spec(wc.shape), _const_spec(wm.shape), _const_spec(wo.shape),
                  _const_spec((1, d)), _const_spec(wr.shape), _const_spec((1, ne)), _const_spec((1, ne))],
        out_specs=[row(d), row(d), row(LANES), row(LANES), row(LANES), pl.BlockSpec((1, ne), lambda i: (0, 0))],
        out_shape=[jax.ShapeDtypeStruct((r, d), F32), jax.ShapeDtypeStruct((r, d), F32),
                   jax.ShapeDtypeStruct((r, LANES), I32), jax.ShapeDtypeStruct((r, LANES), F32),
                   jax.ShapeDtypeStruct((r, LANES), I32), jax.ShapeDtypeStruct((1, ne), F32)],
        scratch_shapes=[pltpu.VMEM((1, ne), F32)],
        compiler_params=_cparams(("arbitrary",)),
        name="merge",
    )(x2d, oa, oc, om, gates, wa, wc, wm, wo, g.reshape(1, d), wr, br.reshape(1, ne), counts_in)


def _moe_kernel(be_ref, nu_ref, xs_ref, wgu_ref, bgu_ref, wd_ref, bd_ref, o_ref, wgu_b, wd_b):
    i = pl.program_id(0)
    e = be_ref[i]
    prev = be_ref[jnp.maximum(i - 1, 0)]

    @pl.when((i == 0) | (e != prev))
    def _():
        wgu_b[...] = wgu_ref[0].astype(BF16)
        wd_b[...] = wd_ref[0].astype(BF16)

    @pl.when(i < nu_ref[0])
    def _():
        f = wd_b.shape[0]
        gu = _dot(xs_ref[...].astype(BF16), wgu_b[...]) + bgu_ref[0]
        g = jnp.minimum(gu[:, :f], SWIGLU_LIMIT)
        u = jnp.clip(gu[:, f:], -SWIGLU_LIMIT, SWIGLU_LIMIT)
        hdn = g * jax.nn.sigmoid(SWIGLU_ALPHA * g) * (u + 1.0)
        o_ref[...] = _dot(hdn.astype(BF16), wd_b[...]) + bd_ref[0]

    @pl.when(i >= nu_ref[0])
    def _():
        o_ref[...] = jnp.zeros(o_ref.shape, F32)


def _moe_experts(xs, blk_e, n_used, w_gate_up, b_gate_up, w_down, b_down):
    n_rows, d = xs.shape
    ne, _, f2 = w_gate_up.shape
    f = f2 // 2
    te = MOE_ROW_TILE
    grid_spec = pltpu.PrefetchScalarGridSpec(
        num_scalar_prefetch=2,
        grid=(n_rows // te,),
        in_specs=[pl.BlockSpec((te, d), lambda i, be, nu: (i, 0)),
                  pl.BlockSpec((1, d, f2), lambda i, be, nu: (be[i], 0, 0)),
                  pl.BlockSpec((1, 1, f2), lambda i, be, nu: (be[i], 0, 0)),
                  pl.BlockSpec((1, f, d), lambda i, be, nu: (be[i], 0, 0)),
                  pl.BlockSpec((1, 1, d), lambda i, be, nu: (be[i], 0, 0))],
        out_specs=pl.BlockSpec((te, d), lambda i, be, nu: (i, 0)),
        scratch_shapes=[pltpu.VMEM((d, f2), BF16), pltpu.VMEM((f, d), BF16)])
    return pl.pallas_call(
        _moe_kernel,
        grid_spec=grid_spec,
        out_shape=jax.ShapeDtypeStruct((n_rows, d), F32),
        compiler_params=_cparams(("arbitrary",)),
        name="moe_experts",
    )(blk_e, n_used, xs, w_gate_up, b_gate_up.reshape(ne, 1, f2), w_down, b_down.reshape(ne, 1, d))


def _moe_plan(top_e, rank, counts, n_tok):
    ne_tile = MOE_ROW_TILE
    ne = counts.shape[0]
    padded = (counts + ne_tile - 1) // ne_tile * ne_tile
    pad_end = jnp.cumsum(padded)
    pad_start = pad_end - padded
    onehot = top_e[:, :, None] == jnp.arange(ne, dtype=I32)[None, None, :]
    dest = jnp.sum(jnp.where(onehot, pad_start[None, None, :], 0), axis=-1) + rank
    n_tiles = (n_tok * TOP_K) // ne_tile + ne
    tile_row = jnp.arange(n_tiles, dtype=I32) * ne_tile
    blk_e = jnp.minimum(jnp.sum((pad_end[None, :] <= tile_row[:, None]).astype(I32), axis=1), ne - 1)
    n_used = (pad_end[-1] // ne_tile).astype(I32).reshape(1)
    return dest.astype(I32), blk_e.astype(I32), n_used, n_tiles * ne_tile


def _combine_kernel(h_ref, rows_ref, tw_ref, g_ref, y_ref):
    tw = tw_ref[...]
    h = h_ref[...]
    for t in range(rows_ref.shape[0]):
        h = h + tw[:, t:t + 1] * rows_ref[t]
    y_ref[...] = _rms(h, g_ref[...])


def _combine(h, rows, tw, g, *, tm):
    r, d = h.shape
    k = rows.shape[0]
    return pl.pallas_call(
        _combine_kernel,
        grid=(r // tm,),
        in_specs=[pl.BlockSpec((tm, d), lambda i: (i, 0)), pl.BlockSpec((k, tm, d), lambda i: (0, i, 0)),
                  pl.BlockSpec((tm, LANES), lambda i: (i, 0)), _const_spec((1, d))],
        out_specs=pl.BlockSpec((tm, d), lambda i: (i, 0)),
        out_shape=jax.ShapeDtypeStruct((r, d), F32),
        compiler_params=_cparams(("arbitrary",)),
        name="combine",
    )(h, rows, tw, g.reshape(1, d))


def kernel(x_prompt, x_sample, mem_prompt, cache_k, cache_v, page_table, state_conv, cache_mem_k, cache_mem_v,
           norm_mix_g, w_in, w_dw, b_dw, conv_ln_g, conv_ln_b, norm_mem_g, w_mem_kv, w_branch_a, w_branch_c,
           w_branch_m, w_out, norm_ffn_g, w_router, b_router, w_gate_up, b_gate_up, w_down, b_down, norm_final_g):
    bp, seq, d = x_prompt.shape
    db, dseq, _ = x_sample.shape
    assert bp == 1 and dseq == 1
    n_mem, m_heads, m_hd = cache_mem_k.shape[1:]
    mw = m_heads * m_hd
    m_scale = m_hd ** -0.5
    cw = w_dw.shape[1]
    kw = w_dw.shape[0]
    page = cache_k.shape[1]
    n_pages = page_table.shape[1]
    past = n_pages * page
    assert past % MOBA_BLOCK == 0 and cache_k.shape[2:] == (A_HEADS, A_HEAD_DIM)

    w_in_bf = w_in.astype(BF16)
    wa, wc, wm, wo = (w.astype(BF16) for w in (w_branch_a, w_branch_c, w_branch_m, w_out))

    xp = x_prompt.reshape(seq, d)
    kv_mem = _memory_kv(mem_prompt.reshape(n_mem, d), norm_mem_g, w_mem_kv)
    mk_p, mv_p = kv_mem[:, :mw], kv_mem[:, mw:]
    cos_p, sin_p = _rope_tables(jnp.arange(seq, dtype=I32))
    q_p, kf_p, vf_p, hg_p, qm_p, gt_p, kb_p, vt_p, ksum_p = _in_projection(
        xp, norm_mix_g, w_in_bf, cos_p, sin_p, tm=512, cw=cw, mw=mw, m_scale=m_scale, prompt=True)
    oa_p = _moba_prompt(q_p, kb_p, vt_p, ksum_p.reshape(seq // MOBA_BLOCK, A_WIDTH), qt=4)
    oc_p = _conv_prompt(hg_p, jnp.zeros((kw - 1, cw), F32), w_dw, b_dw, conv_ln_g, conv_ln_b,
                        tm=_row_tile(seq, 512))
    om_p = _memattn_prompt(qm_p, mk_p.astype(BF16), mv_p.astype(BF16), tm=_row_tile(seq, 512), hd=m_hd)
    n_exp = w_router.shape[1]
    h_p, hn_p, te_p, tw_p, rk_p, cnt_p = _merge(xp, oa_p, oc_p, om_p, gt_p, wa, wc, wm, wo, norm_ffn_g, w_router,
                                                b_router, jnp.zeros((1, n_exp), F32), tm=_row_tile(seq, 256))

    xs = x_sample.reshape(db, d)
    cos_s, sin_s = _rope_tables(jnp.full((db,), past, I32))
    q_s, kf_s, vf_s, hg_s, qm_s, gt_s = _in_projection(
        xs, norm_mix_g, w_in_bf, cos_s, sin_s, tm=db, cw=cw, mw=mw, m_scale=m_scale, prompt=False)
    cache_kt = cache_k.transpose(0, 2, 3, 1)
    cache_vt = cache_v.transpose(0, 2, 3, 1)
    col = lambda a: a.astype(F32).reshape(db, A_WIDTH, 1)
    q_col = col(q_s)
    top = _block_select(cache_kt, page_table, q_col, pages_per_step=min(32, n_pages))
    ppb = MOBA_BLOCK // page
    pages = page_table[jnp.arange(db)[:, None, None, None],
                       top[:, :, :, 0].transpose(0, 2, 1)[..., None] * ppb + jnp.arange(ppb)]
    oa_s = _moba_step(cache_kt, cache_vt, pages.reshape(-1), q_col, col(kf_s), col(vf_s))
    oa_s = oa_s.reshape(db, A_WIDTH).astype(BF16)
    oc_s = _conv_step(state_conv.transpose(1, 0, 2), hg_s, w_dw, b_dw, conv_ln_g, conv_ln_b)
    om_s = _memattn_step(qm_s, cache_mem_k.reshape(db, n_mem, mw), cache_mem_v.reshape(db, n_mem, mw), hd=m_hd)
    h_s, hn_s, te_s, tw_s, rk_s, cnt = _merge(xs, oa_s, oc_s, om_s, gt_s, wa, wc, wm, wo, norm_ffn_g, w_router,
                                              b_router, cnt_p, tm=db)

    n_tok = seq + db
    hn = jnp.concatenate([hn_p, hn_s], axis=0)
    top_e = jnp.concatenate([te_p[:, :TOP_K], te_s[:, :TOP_K]], axis=0)
    rank = jnp.concatenate([rk_p[:, :TOP_K], rk_s[:, :TOP_K]], axis=0)
    tw = jnp.concatenate([tw_p, tw_s], axis=0)
    h_all = jnp.concatenate([h_p, h_s], axis=0)
    dest, blk_e, n_used, n_rows = _moe_plan(top_e, rank, cnt[0].astype(I32), n_tok)
    row_tok = jnp.zeros((n_rows,), I32).at[dest.reshape(-1)].set(
        jnp.repeat(jnp.arange(n_tok, dtype=I32), TOP_K))
    out_rows = _moe_experts(hn[row_tok], blk_e, n_used, w_gate_up, b_gate_up, w_down, b_down)
    y = _combine(h_all, out_rows[dest.T], tw, norm_final_g, tm=_row_tile(n_tok, 384))

    y_prompt = y[:seq].reshape(1, seq, d)
    y_sample = y[seq:].reshape(db, 1, d)
    k_prompt = kf_p.reshape(1, seq, A_HEADS, A_HEAD_DIM)
    v_prompt = vf_p.reshape(1, seq, A_HEADS, A_HEAD_DIM)
    conv_prompt = hg_p[seq - (kw - 1):].reshape(1, kw - 1, cw)
    mem_k_prompt = mk_p.reshape(1, n_mem, m_heads, m_hd)
    mem_v_prompt = mv_p.reshape(1, n_mem, m_heads, m_hd)
    k_sample = kf_s.reshape(db, 1, A_HEADS, A_HEAD_DIM)
    v_sample = vf_s.reshape(db, 1, A_HEADS, A_HEAD_DIM)
    conv_sample = jnp.concatenate([state_conv[:, 1:], hg_s[:, None, :]], axis=1)
    return (y_prompt, y_sample, k_prompt, v_prompt, conv_prompt, mem_k_prompt, mem_v_prompt,
            k_sample, v_sample, conv_sample)
```

```python
import functools
import math

import numpy as np
import jax
import jax.numpy as jnp
from jax import lax
from jax.experimental import pallas as pl
from jax.experimental.pallas import tpu as pltpu

F32 = jnp.float32
BF16 = jnp.bfloat16
I32 = jnp.int32

A_HEADS = 8
A_HEAD_DIM = 64
A_WIDTH = A_HEADS * A_HEAD_DIM
A_SCALE = A_HEAD_DIM ** -0.5
MOBA_BLOCK = 256
MOBA_TOPK = 3
ROPE_THETA = 10000.0
N_BRANCH = 3
TOP_K = 4
SWIGLU_LIMIT = 7.0
SWIGLU_ALPHA = 1.702
EPS = 1e-5
NEG = -1e30
LOG2E = math.log2(math.e)

LANES = 128
VMEM_LIMIT = 56 * 1024 * 1024
MOE_ROW_TILE = 256
V_AUG = A_HEAD_DIM + 16


def _cparams(sem):
    return pltpu.CompilerParams(dimension_semantics=sem, vmem_limit_bytes=VMEM_LIMIT)


def _row_tile(n, cap):
    t = min(cap, n) // 8 * 8
    while n % t:
        t -= 8
    return t


def _const_spec(shape):
    nd = len(shape)
    return pl.BlockSpec(shape, lambda *_: (0,) * nd, pipeline_mode=pl.Buffered(1))


def _rms(x, g):
    return x * lax.rsqrt(jnp.mean(x * x, axis=-1, keepdims=True) + EPS) * g


def _split_bf16(x):
    hi = x.astype(BF16)
    lo = (x - hi.astype(F32)).astype(BF16)
    return hi, lo


def _dot(a, b):
    return jnp.dot(a, b, preferred_element_type=F32)


def _dot_t(a, b):
    return lax.dot_general(a, b, (((1,), (1,)), ((), ())), preferred_element_type=F32)


def _memkv_kernel(mem_ref, g_ref, w_ref, kv_ref):
    xn = _rms(mem_ref[...], g_ref[...])
    kv_ref[...] = _dot(xn.astype(BF16), w_ref[...].astype(BF16))


def _memory_kv(mem2d, g, w):
    n, d = mem2d.shape
    return pl.pallas_call(
        _memkv_kernel,
        out_shape=jax.ShapeDtypeStruct((n, w.shape[1]), F32),
        compiler_params=_cparams(None),
        name="memory_kv",
    )(mem2d, g.reshape(1, d), w)


def _inproj_kernel(x_ref, g_ref, w_ref, cos_ref, sin_ref, q_ref, kf_ref, vf_ref, h_ref, qm_ref, gt_ref,
                   *prompt_refs, m_scale):
    tm = x_ref.shape[0]
    aw = q_ref.shape[1]
    cw = h_ref.shape[1]
    mw = qm_ref.shape[1]
    xb = _rms(x_ref[...], g_ref[...]).astype(BF16)

    def proj(c0, n):
        return _dot(xb, w_ref[:, c0:c0 + n])

    cos = cos_ref[...]
    sin = sin_ref[...]
    lane = lax.broadcasted_iota(I32, (tm, LANES), 1)
    first_half = (lane % A_HEAD_DIM) < (A_HEAD_DIM // 2)

    def rope(c):
        swapped = jnp.where(first_half, pltpu.roll(c, LANES - A_HEAD_DIM // 2, 1),
                            pltpu.roll(c, A_HEAD_DIM // 2, 1))
        return c * cos + swapped * sin

    q = proj(0, aw)
    k = proj(aw, aw)
    for j in range(aw // LANES):
        sl = slice(j * LANES, (j + 1) * LANES)
        q_ref[:, sl] = rope(q[:, sl]).astype(BF16)
        kf_ref[:, sl] = rope(k[:, sl])
    v = proj(2 * aw, aw)
    vf_ref[...] = v
    if prompt_refs:
        ka_ref, vt_ref, ks_ref = prompt_refs
        row_blk = (pl.program_id(0) * tm + lax.broadcasted_iota(I32, (tm, LANES), 0)) // MOBA_BLOCK
        for j in range(aw // LANES):
            kc = kf_ref[:, j * LANES:(j + 1) * LANES] * (A_SCALE * LOG2E)
            low = lane < A_HEAD_DIM
            even = jnp.where(low, kc, jnp.where(lane - A_HEAD_DIM == row_blk, 1.0, 0.0))
            odd = jnp.where(low, jnp.where(lane == row_blk, 1.0, 0.0), kc)
            ka_ref[:, (2 * j) * LANES:(2 * j + 1) * LANES] = even.astype(BF16)
            ka_ref[:, (2 * j + 1) * LANES:(2 * j + 2) * LANES] = odd.astype(BF16)
        for b in range(tm // MOBA_BLOCK):
            ks_ref[b] = jnp.sum(kf_ref[b * MOBA_BLOCK:(b + 1) * MOBA_BLOCK, :], axis=0, keepdims=True)
        vt = v.T
        ones = jnp.ones((V_AUG - A_HEAD_DIM, tm), BF16)
        for h in range(A_HEADS):
            vt_ref[h * V_AUG:h * V_AUG + A_HEAD_DIM, :] = vt[h * A_HEAD_DIM:(h + 1) * A_HEAD_DIM, :].astype(BF16)
            vt_ref[h * V_AUG + A_HEAD_DIM:(h + 1) * V_AUG, :] = ones
    a = proj(3 * aw, cw)
    gate = proj(3 * aw + cw, cw)
    h_ref[...] = a * jax.nn.sigmoid(gate)
    c0 = 3 * aw + 2 * cw
    qm_ref[...] = (proj(c0, mw) * m_scale).astype(BF16)
    c0 += mw
    gw = gt_ref.shape[1]
    step = 512
    for j in range(gw // step):
        gt_ref[:, j * step:(j + 1) * step] = jax.nn.sigmoid(proj(c0 + j * step, step))


def _in_projection(x2d, g, w_bf, cos, sin, *, tm, cw, mw, m_scale, prompt):
    r, d = x2d.shape
    aw = A_WIDTH
    gw = w_bf.shape[1] - (3 * aw + 2 * cw + mw)
    row = lambda width: pl.BlockSpec((tm, width), lambda i: (i, 0))
    out_shape = [jax.ShapeDtypeStruct((r, aw), BF16), jax.ShapeDtypeStruct((r, aw), F32),
                 jax.ShapeDtypeStruct((r, aw), F32), jax.ShapeDtypeStruct((r, cw), F32),
                 jax.ShapeDtypeStruct((r, mw), BF16), jax.ShapeDtypeStruct((r, gw), F32)]
    out_specs = [row(aw), row(aw), row(aw), row(cw), row(mw), row(gw)]
    if prompt:
        assert tm % MOBA_BLOCK == 0
        bpt = tm // MOBA_BLOCK
        out_shape += [jax.ShapeDtypeStruct((r, A_HEADS * LANES), BF16),
                      jax.ShapeDtypeStruct((A_HEADS * V_AUG, r), BF16),
                      jax.ShapeDtypeStruct((r // MOBA_BLOCK, 1, aw), F32)]
        out_specs += [row(A_HEADS * LANES), pl.BlockSpec((A_HEADS * V_AUG, tm), lambda i: (0, i)),
                      pl.BlockSpec((bpt, 1, aw), lambda i: (i, 0, 0))]
    return pl.pallas_call(
        functools.partial(_inproj_kernel, m_scale=m_scale),
        grid=(r // tm,),
        in_specs=[row(d), _const_spec((1, d)), _const_spec(w_bf.shape), row(LANES), row(LANES)],
        out_specs=out_specs,
        out_shape=out_shape,
        compiler_params=_cparams(("arbitrary",)),
        name="in_projection",
    )(x2d, g.reshape(1, d), w_bf, cos, sin)


def _rope_tables(pos):
    half = A_HEAD_DIM // 2
    inv = ROPE_THETA ** (-jnp.arange(half, dtype=F32) / half)
    ang = pos.astype(F32)[:, None] * inv[None, :]
    cos = jnp.cos(ang)
    sin = jnp.sin(ang)
    reps = LANES // A_HEAD_DIM
    return (jnp.tile(jnp.concatenate([cos, cos], axis=-1), (1, reps)),
            jnp.tile(jnp.concatenate([-sin, sin], axis=-1), (1, reps)))


def _moba_kernel(qi_ref, ki_ref, q_ref, k_ref, vt_ref, ks_ref, o_ref, qaug, m_scr, acc, st_scr, *, qt):
    step = pl.program_id(0)
    qi = qi_ref[step]
    kb = ki_ref[step]
    dh = A_HEAD_DIM
    blk = MOBA_BLOCK
    n_chunks = A_WIDTH // LANES

    @pl.when(kb == 0)
    def _select():
        m_scr[...] = jnp.full(m_scr.shape, -jnp.inf, F32)
        acc[...] = jnp.zeros(acc.shape, F32)
        lane_k = lax.broadcasted_iota(I32, (dh, LANES), 1)
        rowid = lax.broadcasted_iota(I32, (dh, blk), 0)
        for s in range(qt):
            j = qi * qt + s
            qT = q_ref[s * blk:(s + 1) * blk, :].astype(F32).T
            for c in range(n_chunks):
                qTc = qT[c * LANES:(c + 1) * LANES, :].astype(BF16)
                ksc = ks_ref[:, c * LANES:(c + 1) * LANES]
                for par in range(2):
                    h = 2 * c + par
                    mine = (lane_k < dh) if par == 0 else (lane_k >= dh)
                    sc = _dot(jnp.where(mine, ksc, 0.0).astype(BF16), qTc)
                    sc = jnp.where(rowid < j, sc, NEG)
                    sel = rowid == j
                    for t in range(MOBA_TOPK):
                        mx = jnp.max(sc, axis=0, keepdims=True)
                        idx = jnp.min(jnp.where(sc == mx, rowid, dh), axis=0, keepdims=True)
                        pick = rowid == idx
                        sel = sel | (pick & (t < j))
                        sc = jnp.where(pick, -jnp.inf, sc)
                    bias = jnp.where(sel, 0.0, NEG).astype(BF16)
                    qh = qTc[par * dh:(par + 1) * dh, :]
                    qaug[s * A_HEADS + h, par * dh:(par + 1) * dh, :] = qh
                    qaug[s * A_HEADS + h, (1 - par) * dh:(2 - par) * dh, :] = bias

    def sweep(subs, masked):
        if masked:
            causal = lax.broadcasted_iota(I32, (blk, blk), 0) <= lax.broadcasted_iota(I32, (blk, blk), 1)
        chains = [(s, h) for s in subs for h in range(A_HEADS)]
        slots = st_scr.shape[0]

        def logits(i):
            s, h = chains[i]
            st_scr[i % slots] = _dot(k_ref[:, h * LANES:(h + 1) * LANES], qaug[s * A_HEADS + h])

        for i in range(min(slots - 1, len(chains))):
            logits(i)
        for i, (s, h) in enumerate(chains):
            if i + slots - 1 < len(chains):
                logits(i + slots - 1)
            a = s * A_HEADS + h
            st = st_scr[i % slots]
            if masked:
                st = jnp.where(causal, st, NEG)
            m_prev = m_scr[s, h:h + 1, :]
            m_new = jnp.maximum(m_prev, jnp.max(st, axis=0, keepdims=True))
            p = jnp.exp2(st - m_new).astype(BF16)
            acc[a] = jnp.exp2(m_prev - m_new) * acc[a] + _dot(vt_ref[h * V_AUG:(h + 1) * V_AUG, :], p)
            m_scr[s, h:h + 1, :] = m_new

    pl.when(kb < qi * qt)(functools.partial(sweep, list(range(qt)), False))
    for s in range(qt):
        j = qi * qt + s
        pl.when((kb >= qi * qt) & (kb < j))(functools.partial(sweep, [s], False))
        pl.when(kb == j)(functools.partial(sweep, [s], True))

    @pl.when(kb == qi * qt + qt - 1)
    def _finish():
        for s in range(qt):
            outs = []
            for h in range(A_HEADS):
                a = acc[s * A_HEADS + h]
                outs.append(a[0:dh, :] / a[dh:dh + 1, :])
            o_ref[s * blk:(s + 1) * blk, :] = jnp.concatenate(outs, axis=0).T.astype(BF16)


def _moba_prompt(q, ka, vt, ksum, *, qt):
    r, aw = q.shape
    nb = r // MOBA_BLOCK
    assert r % (qt * MOBA_BLOCK) == 0 and MOBA_TOPK <= nb <= A_HEAD_DIM and A_HEAD_DIM * 2 == LANES
    tq = qt * MOBA_BLOCK
    ks = jnp.zeros((A_HEAD_DIM, aw), F32).at[:nb].set(ksum)
    qi, ki = [], []
    for i in range(r // tq):
        for n in range((i + 1) * qt):
            qi.append(i)
            ki.append(n)
    grid_spec = pltpu.PrefetchScalarGridSpec(
        num_scalar_prefetch=2,
        grid=(len(qi),),
        in_specs=[pl.BlockSpec((tq, aw), lambda s, qi, ki: (qi[s], 0)),
                  pl.BlockSpec((MOBA_BLOCK, A_HEADS * LANES), lambda s, qi, ki: (ki[s], 0)),
                  pl.BlockSpec((A_HEADS * V_AUG, MOBA_BLOCK), lambda s, qi, ki: (0, ki[s])),
                  pl.BlockSpec(ks.shape, lambda s, qi, ki: (0, 0))],
        out_specs=pl.BlockSpec((tq, aw), lambda s, qi, ki: (qi[s], 0)),
        scratch_shapes=[pltpu.VMEM((qt * A_HEADS, LANES, MOBA_BLOCK), BF16),
                        pltpu.VMEM((qt, A_HEADS, MOBA_BLOCK), F32),
                        pltpu.VMEM((qt * A_HEADS, V_AUG, MOBA_BLOCK), F32),
                        pltpu.VMEM((9, MOBA_BLOCK, MOBA_BLOCK), F32)])
    return pl.pallas_call(
        functools.partial(_moba_kernel, qt=qt),
        grid_spec=grid_spec,
        out_shape=jax.ShapeDtypeStruct((r, aw), BF16),
        compiler_params=_cparams(("arbitrary",)),
        name="moba_prompt",
    )(jnp.asarray(np.array(qi, np.int32)), jnp.asarray(np.array(ki, np.int32)), q, ka, vt, ks)


def _ln_silu(y, g, b):
    mu = jnp.mean(y, axis=-1, keepdims=True)
    d = y - mu
    var = jnp.mean(d * d, axis=-1, keepdims=True)
    z = d * lax.rsqrt(var + EPS) * g + b
    return z * jax.nn.sigmoid(z)


def _conv_prompt_kernel(h_ref, st_ref, w_ref, b_ref, g_ref, beta_ref, o_ref, ext, *, halo, rc):
    tm = h_ref.shape[0]
    kw = w_ref.shape[0]

    @pl.when(pl.program_id(0) == 0)
    def _():
        ext[0:halo, :] = st_ref[...]

    ext[halo:halo + tm, :] = h_ref[...]
    off = halo - (kw - 1)
    for r0 in range(0, tm, rc):
        y = jnp.zeros((rc, h_ref.shape[1]), F32) + b_ref[...]
        for k in range(kw):
            y = y + w_ref[k:k + 1, :] * ext[r0 + off + k:r0 + off + k + rc, :]
        o_ref[r0:r0 + rc, :] = _ln_silu(y, g_ref[...], beta_ref[...]).astype(BF16)
    ext[0:halo, :] = ext[tm:tm + halo, :]


def _conv_prompt(h2d, state, w_dw, b_dw, ln_g, ln_b, *, tm):
    r, c = h2d.shape
    kw = w_dw.shape[0]
    halo = -(-(kw - 1) // 8) * 8
    st = jnp.zeros((halo, c), F32).at[halo - (kw - 1):].set(state)
    vec = lambda a: a.reshape(1, c)
    return pl.pallas_call(
        functools.partial(_conv_prompt_kernel, halo=halo, rc=64),
        grid=(r // tm,),
        in_specs=[pl.BlockSpec((tm, c), lambda i: (i, 0)), _const_spec((halo, c)), _const_spec((kw, c)),
                  _const_spec((1, c)), _const_spec((1, c)), _const_spec((1, c))],
        out_specs=pl.BlockSpec((tm, c), lambda i: (i, 0)),
        out_shape=jax.ShapeDtypeStruct((r, c), BF16),
        scratch_shapes=[pltpu.VMEM((tm + halo, c), F32)],
        compiler_params=_cparams(("arbitrary",)),
        name="conv_prompt",
    )(h2d, st, w_dw, vec(b_dw), vec(ln_g), vec(ln_b))


def _conv_step_kernel(st_ref, h_ref, w_ref, b_ref, g_ref, beta_ref, o_ref):
    kw = w_ref.shape[0]
    y = b_ref[...] + w_ref[kw - 1:kw, :] * h_ref[...]
    for k in range(kw - 1):
        y = y + w_ref[k:k + 1, :] * st_ref[k]
    o_ref[...] = _ln_silu(y, g_ref[...], beta_ref[...]).astype(BF16)


def _conv_step(state_t, h2d, w_dw, b_dw, ln_g, ln_b):
    b, c = h2d.shape
    vec = lambda a: a.reshape(1, c)
    return pl.pallas_call(
        _conv_step_kernel,
        out_shape=jax.ShapeDtypeStruct((b, c), BF16),
        compiler_params=_cparams(None),
        name="conv_step",
    )(state_t, h2d, w_dw, vec(b_dw), vec(ln_g), vec(ln_b))


def _memattn_kernel(q_ref, k_ref, v_ref, o_ref, *, hd):
    for h in range(q_ref.shape[1] // hd):
        sl = slice(h * hd, (h + 1) * hd)
        s = _dot_t(q_ref[:, sl], k_ref[:, sl])
        p = jnp.exp(s - jnp.max(s, axis=-1, keepdims=True))
        inv = 1.0 / jnp.sum(p, axis=-1, keepdims=True)
        o_ref[:, sl] = (_dot(p.astype(BF16), v_ref[:, sl]) * inv).astype(BF16)


def _memattn_prompt(qm, mk_bf, mv_bf, *, tm, hd):
    r, w = qm.shape
    return pl.pallas_call(
        functools.partial(_memattn_kernel, hd=hd),
        grid=(r // tm,),
        in_specs=[pl.BlockSpec((tm, w), lambda i: (i, 0)), _const_spec(mk_bf.shape), _const_spec(mv_bf.shape)],
        out_specs=pl.BlockSpec((tm, w), lambda i: (i, 0)),
        out_shape=jax.ShapeDtypeStruct((r, w), BF16),
        compiler_params=_cparams(("arbitrary",)),
        name="memattn_prompt",
    )(qm, mk_bf, mv_bf)


def _memattn_step_kernel(q_ref, k_ref, v_ref, o_ref):
    nh, hd = k_ref.shape[2:]
    for h in range(nh):
        lanes = slice(h * hd, (h + 1) * hd)
        q = q_ref[0, :, lanes].astype(F32)
        s = jnp.sum(k_ref[0, :, h, :] * q, axis=-1, keepdims=True)
        p = jnp.exp(s - jnp.max(s, axis=0, keepdims=True))
        num = jnp.sum(p * v_ref[0, :, h, :], axis=0, keepdims=True)
        o_ref[0, :, lanes] = (num / jnp.sum(p, axis=0, keepdims=True)).astype(BF16)


def _memattn_step(q, k, v):
    b, w = q.shape
    nk, nh, hd = k.shape[1:]
    one = pl.BlockSpec((1, 1, w), lambda i: (i, 0, 0))
    many = pl.BlockSpec((1, nk, nh, hd), lambda i: (i, 0, 0, 0))
    out = pl.pallas_call(
        _memattn_step_kernel,
        grid=(b,),
        in_specs=[one, many, many],
        out_specs=one,
        out_shape=jax.ShapeDtypeStruct((b, 1, w), BF16),
        compiler_params=_cparams(("arbitrary",)),
        name="memattn_step",
    )(q.reshape(b, 1, w), k, v)
    return out.reshape(b, w)


def _block_select_kernel(pt_ref, q_ref, *refs, pages_per_block, n_blocks):
    del pt_ref
    page_refs = refs[:-3]
    top_ref, qb, sc = refs[-3:]
    s = pl.program_id(1)
    nh, dh, page = page_refs[0].shape[1:]
    bps = len(page_refs) // pages_per_block
    sub = 8

    @pl.when(s == 0)
    def _():
        qb[...] = jnp.broadcast_to(q_ref[0], qb.shape)

    for m in range(bps):
        part = None
        for j in range(pages_per_block):
            x = page_refs[m * pages_per_block + j][0].reshape(nh * dh, page) * qb[...]
            x = jnp.sum(x.reshape(nh, dh // sub, sub, page), axis=1)
            part = x if part is None else part + x
        tot = jnp.sum(jnp.sum(part, axis=2, keepdims=True), axis=1)
        sc[s * bps + m] = tot * (1.0 / MOBA_BLOCK)

    @pl.when(s == pl.num_programs(1) - 1)
    def _():
        scores = sc[...]
        ids = lax.broadcasted_iota(I32, scores.shape, 0)
        for t in range(MOBA_TOPK):
            mx = jnp.max(scores, axis=0, keepdims=True)
            idx = jnp.min(jnp.where(scores == mx, ids, n_blocks), axis=0, keepdims=True)
            top_ref[0, t] = idx[0]
            scores = jnp.where(ids == idx, -jnp.inf, scores)


def _block_select(cache_kt, page_table, q_col, *, pages_per_step):
    db, n_pages = page_table.shape
    _, nh, dh, page = cache_kt.shape
    ppb = MOBA_BLOCK // page
    n_blocks = n_pages // ppb
    assert n_pages % pages_per_step == 0 and pages_per_step % ppb == 0 and n_blocks >= MOBA_TOPK

    def page_spec(j):
        return pl.BlockSpec((1, nh, dh, page), lambda b, s, pt: (pt[b, s * pages_per_step + j], 0, 0, 0))

    grid_spec = pltpu.PrefetchScalarGridSpec(
        num_scalar_prefetch=1,
        grid=(db, n_pages // pages_per_step),
        in_specs=[pl.BlockSpec((1, nh * dh, 1), lambda b, s, pt: (b, 0, 0))]
        + [page_spec(j) for j in range(pages_per_step)],
        out_specs=pl.BlockSpec((1, MOBA_TOPK, nh, 1), lambda b, s, pt: (b, 0, 0, 0)),
        scratch_shapes=[pltpu.VMEM((nh * dh, page), F32), pltpu.VMEM((n_blocks, nh, 1), F32)])
    return pl.pallas_call(
        functools.partial(_block_select_kernel, pages_per_block=ppb, n_blocks=n_blocks),
        grid_spec=grid_spec,
        out_shape=jax.ShapeDtypeStruct((db, MOBA_TOPK, nh, 1), I32),
        compiler_params=_cparams(("arbitrary", "arbitrary")),
        name="block_select",
    )(page_table, q_col, *([cache_kt] * pages_per_step))


def _moba_step_kernel(pg_ref, q_ref, kn_ref, vn_ref, kt_hbm, vt_hbm, o_ref, kbuf, vbuf, sem, *, n_tiles):
    b = pl.program_id(0)
    dh = A_HEAD_DIM

    def tile_copies(seq, slot):
        out = []
        for h in range(A_HEADS):
            for t in range(n_tiles):
                pg = pg_ref[(seq * A_HEADS + h) * n_tiles + t]
                out.append(pltpu.make_async_copy(kt_hbm.at[pg, h], kbuf.at[slot, h, t], sem.at[slot, 0]))
                out.append(pltpu.make_async_copy(vt_hbm.at[pg, h], vbuf.at[slot, h, t], sem.at[slot, 1]))
        return out

    @pl.when(b == 0)
    def _():
        for cp in tile_copies(0, 0):
            cp.start()

    @pl.when(b + 1 < pl.num_programs(0))
    def _():
        for cp in tile_copies(b + 1, (b + 1) % 2):
            cp.start()

    slot = b % 2
    for cp in tile_copies(b, slot):
        cp.wait()

    for h in range(A_HEADS):
        rows = slice(h * dh, (h + 1) * dh)
        q = q_ref[0, rows, :]
        s_new = jnp.sum(q * kn_ref[0, rows, :], axis=0, keepdims=True) * A_SCALE
        ss = [jnp.sum(kbuf[slot, h, t] * q, axis=0, keepdims=True) * A_SCALE
              for t in range(n_tiles)]
        m = s_new
        for s in ss:
            m = jnp.maximum(m, jnp.max(s, axis=-1, keepdims=True))
        p_new = jnp.exp(s_new - m)
        den = p_new
        num = p_new * vn_ref[0, rows, :]
        acc = None
        for t in range(n_tiles):
            p = jnp.exp(ss[t] - m)
            den = den + jnp.sum(p, axis=-1, keepdims=True)
            pv = vbuf[slot, h, t] * p
            acc = pv if acc is None else acc + pv
        num = num + jnp.sum(acc, axis=-1, keepdims=True)
        o_ref[0, rows, :] = num / den


def _moba_step(cache_kt, cache_vt, pages, q_col, kn_col, vn_col):
    db, w, _ = q_col.shape
    _, nh, dh, page = cache_kt.shape
    n_tiles = pages.shape[0] // (db * nh)
    col = pl.BlockSpec((1, w, 1), lambda b, pg: (b, 0, 0))
    hbm = pl.BlockSpec(memory_space=pl.ANY)
    grid_spec = pltpu.PrefetchScalarGridSpec(
        num_scalar_prefetch=1,
        grid=(db,),
        in_specs=[col, col, col, hbm, hbm],
        out_specs=col,
        scratch_shapes=[pltpu.VMEM((2, nh, n_tiles, dh, page), F32), pltpu.VMEM((2, nh, n_tiles, dh, page), F32),
                        pltpu.SemaphoreType.DMA((2, 2))])
    return pl.pallas_call(
        functools.partial(_moba_step_kernel, n_tiles=n_tiles),
        grid_spec=grid_spec,
        out_shape=jax.ShapeDtypeStruct((db, w, 1), F32),
        compiler_params=_cparams(("arbitrary",)),
        name="moba_step",
    )(pages, q_col, kn_col, vn_col, cache_kt, cache_vt)


def _merge_kernel(x_ref, oa_ref, oc_ref, om_ref, gt_ref, wa_ref, wc_ref, wm_ref, wo_ref, g_ref, wr_ref, br_ref,
                  cin_ref, h_ref, hn_ref, te_ref, tw_ref, rk_ref, cnt_ref, carry):
    d = x_ref.shape[1]
    tm = x_ref.shape[0]
    merged = (gt_ref[:, 0:d] * _dot(oa_ref[...], wa_ref[...])
              + gt_ref[:, d:2 * d] * _dot(oc_ref[...], wc_ref[...])
              + gt_ref[:, 2 * d:3 * d] * _dot(om_ref[...], wm_ref[...]))
    h = x_ref[...] + _dot(merged.astype(BF16), wo_ref[...])
    h_ref[...] = h
    hn = _rms(h, g_ref[...])
    hn_ref[...] = hn
    hi, lo = _split_bf16(hn)
    whi, wlo = _split_bf16(wr_ref[...])
    logits = _dot(hi, whi) + _dot(lo, whi) + _dot(hi, wlo) + br_ref[...]
    ne = logits.shape[1]
    lane = lax.broadcasted_iota(I32, (tm, ne), 1)
    out_lane = lax.broadcasted_iota(I32, (tm, LANES), 1)
    s = logits
    vals, picks, te = [], [], jnp.zeros((tm, LANES), I32)
    for t in range(TOP_K):
        mx = jnp.max(s, axis=-1, keepdims=True)
        idx = jnp.min(jnp.where(s == mx, lane, ne), axis=-1, keepdims=True)
        vals.append(mx)
        picks.append(lane == idx)
        te = jnp.where(out_lane == t, idx, te)
        s = jnp.where(picks[-1], -jnp.inf, s)
    es = [jnp.exp(v - vals[0]) for v in vals]
    inv = 1.0 / functools.reduce(lambda a, b: a + b, es)
    tw = jnp.zeros((tm, LANES), F32)
    for t in range(TOP_K):
        tw = jnp.where(out_lane == t, es[t] * inv, tw)
    te_ref[...] = te
    tw_ref[...] = tw

    @pl.when(pl.program_id(0) == 0)
    def _():
        carry[...] = cin_ref[...]

    chosen = functools.reduce(lambda a, b: a | b, picks)
    earlier = lax.broadcasted_iota(I32, (tm, tm), 1) < lax.broadcasted_iota(I32, (tm, tm), 0)
    before = carry[...] + _dot(jnp.where(earlier, 1.0, 0.0).astype(BF16), jnp.where(chosen, 1.0, 0.0).astype(BF16))
    rk = jnp.zeros((tm, LANES), I32)
    for t in range(TOP_K):
        r_t = jnp.sum(jnp.where(picks[t], before, 0.0), axis=-1, keepdims=True)
        rk = jnp.where(out_lane == t, r_t.astype(I32), rk)
    rk_ref[...] = rk
    carry[...] = carry[...] + jnp.sum(jnp.where(chosen, 1.0, 0.0), axis=0, keepdims=True)
    cnt_ref[...] = carry[...]


def _merge(x2d, oa, oc, om, gates, wa, wc, wm, wo, g, wr, br, counts_in, *, tm):
    r, d = x2d.shape
    ne = wr.shape[1]
    row = lambda width: pl.BlockSpec((tm, width), lambda i: (i, 0))
    return pl.pallas_call(
        _merge_kernel,
        grid=(r // tm,),
        in_specs=[row(d), row(oa.shape[1]), row(oc.shape[1]), row(om.shape[1]), row(gates.shape[1]),
                  _const_spec(wa.shape), _const_spec(wc.shape), _const_spec(wm.shape), _const_spec(wo.shape),
                  _const_spec((1, d)), _const_spec(wr.shape), _const_spec((1, ne)), _const_spec((1, ne))],
        out_specs=[row(d), row(d), row(LANES), row(LANES), row(LANES), pl.BlockSpec((1, ne), lambda i: (0, 0))],
        out_shape=[jax.ShapeDtypeStruct((r, d), F32), jax.ShapeDtypeStruct((r, d), F32),
                   jax.ShapeDtypeStruct((r, LANES), I32), jax.ShapeDtypeStruct((r, LANES), F32),
                   jax.ShapeDtypeStruct((r, LANES), I32), jax.ShapeDtypeStruct((1, ne), F32)],
        scratch_shapes=[pltpu.VMEM((1, ne), F32)],
        compiler_params=_cparams(("arbitrary",)),
        name="merge",
    )(x2d, oa, oc, om, gates, wa, wc, wm, wo, g.reshape(1, d), wr, br.reshape(1, ne), counts_in)


def _moe_kernel(be_ref, nu_ref, xs_ref, wgu_ref, bgu_ref, wd_ref, bd_ref, o_ref, wgu_b, wd_b):
    i = pl.program_id(0)
    e = be_ref[i]
    prev = be_ref[jnp.maximum(i - 1, 0)]

    @pl.when((i == 0) | (e != prev))
    def _():
        wgu_b[...] = wgu_ref[0].astype(BF16)
        wd_b[...] = wd_ref[0].astype(BF16)

    @pl.when(i < nu_ref[0])
    def _():
        f = wd_b.shape[0]
        gu = _dot(xs_ref[...].astype(BF16), wgu_b[...]) + bgu_ref[0]
        g = jnp.minimum(gu[:, :f], SWIGLU_LIMIT)
        u = jnp.clip(gu[:, f:], -SWIGLU_LIMIT, SWIGLU_LIMIT)
        hdn = g * jax.nn.sigmoid(SWIGLU_ALPHA * g) * (u + 1.0)
        o_ref[...] = _dot(hdn.astype(BF16), wd_b[...]) + bd_ref[0]

    @pl.when(i >= nu_ref[0])
    def _():
        o_ref[...] = jnp.zeros(o_ref.shape, F32)


def _moe_experts(xs, blk_e, n_used, w_gate_up, b_gate_up, w_down, b_down):
    n_rows, d = xs.shape
    ne, _, f2 = w_gate_up.shape
    f = f2 // 2
    te = MOE_ROW_TILE
    grid_spec = pltpu.PrefetchScalarGridSpec(
        num_scalar_prefetch=2,
        grid=(n_rows // te,),
        in_specs=[pl.BlockSpec((te, d), lambda i, be, nu: (i, 0)),
                  pl.BlockSpec((1, d, f2), lambda i, be, nu: (be[i], 0, 0)),
                  pl.BlockSpec((1, 1, f2), lambda i, be, nu: (be[i], 0, 0)),
                  pl.BlockSpec((1, f, d), lambda i, be, nu: (be[i], 0, 0)),
                  pl.BlockSpec((1, 1, d), lambda i, be, nu: (be[i], 0, 0))],
        out_specs=pl.BlockSpec((te, d), lambda i, be, nu: (i, 0)),
        scratch_shapes=[pltpu.VMEM((d, f2), BF16), pltpu.VMEM((f, d), BF16)])
    return pl.pallas_call(
        _moe_kernel,
        grid_spec=grid_spec,
        out_shape=jax.ShapeDtypeStruct((n_rows, d), F32),
        compiler_params=_cparams(("arbitrary",)),
        name="moe_experts",
    )(blk_e, n_used, xs, w_gate_up, b_gate_up.reshape(ne, 1, f2), w_down, b_down.reshape(ne, 1, d))


def _moe_plan(top_e, rank, counts, n_tok):
    ne_tile = MOE_ROW_TILE
    ne = counts.shape[0]
    padded = (counts + ne_tile - 1) // ne_tile * ne_tile
    pad_end = jnp.cumsum(padded)
    pad_start = pad_end - padded
    onehot = top_e[:, :, None] == jnp.arange(ne, dtype=I32)[None, None, :]
    dest = jnp.sum(jnp.where(onehot, pad_start[None, None, :], 0), axis=-1) + rank
    n_tiles = (n_tok * TOP_K) // ne_tile + ne
    tile_row = jnp.arange(n_tiles, dtype=I32) * ne_tile
    blk_e = jnp.minimum(jnp.sum((pad_end[None, :] <= tile_row[:, None]).astype(I32), axis=1), ne - 1)
    n_used = (pad_end[-1] // ne_tile).astype(I32).reshape(1)
    return dest.astype(I32), blk_e.astype(I32), n_used, n_tiles * ne_tile


def _combine_kernel(h_ref, rows_ref, tw_ref, g_ref, y_ref):
    tw = tw_ref[...]
    h = h_ref[...]
    for t in range(rows_ref.shape[0]):
        h = h + tw[:, t:t + 1] * rows_ref[t]
    y_ref[...] = _rms(h, g_ref[...])


def _combine(h, rows, tw, g, *, tm):
    r, d = h.shape
    k = rows.shape[0]
    return pl.pallas_call(
        _combine_kernel,
        grid=(r // tm,),
        in_specs=[pl.BlockSpec((tm, d), lambda i: (i, 0)), pl.BlockSpec((k, tm, d), lambda i: (0, i, 0)),
                  pl.BlockSpec((tm, LANES), lambda i: (i, 0)), _const_spec((1, d))],
        out_specs=pl.BlockSpec((tm, d), lambda i: (i, 0)),
        out_shape=jax.ShapeDtypeStruct((r, d), F32),
        compiler_params=_cparams(("arbitrary",)),
        name="combine",
    )(h, rows, tw, g.reshape(1, d))


def kernel(x_prompt, x_sample, mem_prompt, cache_k, cache_v, page_table, state_conv, cache_mem_k, cache_mem_v,
           norm_mix_g, w_in, w_dw, b_dw, conv_ln_g, conv_ln_b, norm_mem_g, w_mem_kv, w_branch_a, w_branch_c,
           w_branch_m, w_out, norm_ffn_g, w_router, b_router, w_gate_up, b_gate_up, w_down, b_down, norm_final_g):
    bp, seq, d = x_prompt.shape
    db, dseq, _ = x_sample.shape
    assert bp == 1 and dseq == 1
    n_mem, m_heads, m_hd = cache_mem_k.shape[1:]
    mw = m_heads * m_hd
    m_scale = m_hd ** -0.5
    cw = w_dw.shape[1]
    kw = w_dw.shape[0]
    page = cache_k.shape[1]
    n_pages = page_table.shape[1]
    past = n_pages * page
    assert past % MOBA_BLOCK == 0 and cache_k.shape[2:] == (A_HEADS, A_HEAD_DIM)

    w_in_bf = w_in.astype(BF16)
    wa, wc, wm, wo = (w.astype(BF16) for w in (w_branch_a, w_branch_c, w_branch_m, w_out))

    xp = x_prompt.reshape(seq, d)
    kv_mem = _memory_kv(mem_prompt.reshape(n_mem, d), norm_mem_g, w_mem_kv)
    mk_p, mv_p = kv_mem[:, :mw], kv_mem[:, mw:]
    cos_p, sin_p = _rope_tables(jnp.arange(seq, dtype=I32))
    q_p, kf_p, vf_p, hg_p, qm_p, gt_p, kb_p, vt_p, ksum_p = _in_projection(
        xp, norm_mix_g, w_in_bf, cos_p, sin_p, tm=512, cw=cw, mw=mw, m_scale=m_scale, prompt=True)
    oa_p = _moba_prompt(q_p, kb_p, vt_p, ksum_p.reshape(seq // MOBA_BLOCK, A_WIDTH), qt=4)
    oc_p = _conv_prompt(hg_p, jnp.zeros((kw - 1, cw), F32), w_dw, b_dw, conv_ln_g, conv_ln_b,
                        tm=_row_tile(seq, 512))
    om_p = _memattn_prompt(qm_p, mk_p.astype(BF16), mv_p.astype(BF16), tm=_row_tile(seq, 512), hd=m_hd)
    n_exp = w_router.shape[1]
    h_p, hn_p, te_p, tw_p, rk_p, cnt_p = _merge(xp, oa_p, oc_p, om_p, gt_p, wa, wc, wm, wo, norm_ffn_g, w_router,
                                                b_router, jnp.zeros((1, n_exp), F32), tm=_row_tile(seq, 512))

    xs = x_sample.reshape(db, d)
    cos_s, sin_s = _rope_tables(jnp.full((db,), past, I32))
    q_s, kf_s, vf_s, hg_s, qm_s, gt_s = _in_projection(
        xs, norm_mix_g, w_in_bf, cos_s, sin_s, tm=db, cw=cw, mw=mw, m_scale=m_scale, prompt=False)
    cache_kt = cache_k.transpose(0, 2, 3, 1)
    cache_vt = cache_v.transpose(0, 2, 3, 1)
    col = lambda a: a.astype(F32).reshape(db, A_WIDTH, 1)
    q_col = col(q_s)
    top = _block_select(cache_kt, page_table, q_col, pages_per_step=min(32, n_pages))
    ppb = MOBA_BLOCK // page
    pages = page_table[jnp.arange(db)[:, None, None, None],
                       top[:, :, :, 0].transpose(0, 2, 1)[..., None] * ppb + jnp.arange(ppb)]
    oa_s = _moba_step(cache_kt, cache_vt, pages.reshape(-1), q_col, col(kf_s), col(vf_s))
    oa_s = oa_s.reshape(db, A_WIDTH).astype(BF16)
    oc_s = _conv_step(state_conv.transpose(1, 0, 2), hg_s, w_dw, b_dw, conv_ln_g, conv_ln_b)
    om_s = _memattn_step(qm_s, cache_mem_k, cache_mem_v)
    h_s, hn_s, te_s, tw_s, rk_s, cnt = _merge(xs, oa_s, oc_s, om_s, gt_s, wa, wc, wm, wo, norm_ffn_g, w_router,
                                              b_router, cnt_p, tm=db)

    n_tok = seq + db
    hn = jnp.concatenate([hn_p, hn_s], axis=0)
    top_e = jnp.concatenate([te_p[:, :TOP_K], te_s[:, :TOP_K]], axis=0)
    rank = jnp.concatenate([rk_p[:, :TOP_K], rk_s[:, :TOP_K]], axis=0)
    tw = jnp.concatenate([tw_p, tw_s], axis=0)
    h_all = jnp.concatenate([h_p, h_s], axis=0)
    dest, blk_e, n_used, n_rows = _moe_plan(top_e, rank, cnt[0].astype(I32), n_tok)
    row_tok = jnp.zeros((n_rows,), I32).at[dest.reshape(-1)].set(
        jnp.repeat(jnp.arange(n_tok, dtype=I32), TOP_K))
    out_rows = _moe_experts(hn[row_tok], blk_e, n_used, w_gate_up, b_gate_up, w_down, b_down)
    y = _combine(h_all, out_rows[dest.T], tw, norm_final_g, tm=_row_tile(n_tok, 384))

    y_prompt = y[:seq].reshape(1, seq, d)
    y_sample = y[seq:].reshape(db, 1, d)
    k_prompt = kf_p.reshape(1, seq, A_HEADS, A_HEAD_DIM)
    v_prompt = vf_p.reshape(1, seq, A_HEADS, A_HEAD_DIM)
    conv_prompt = hg_p[seq - (kw - 1):].reshape(1, kw - 1, cw)
    mem_k_prompt = mk_p.reshape(1, n_mem, m_heads, m_hd)
    mem_v_prompt = mv_p.reshape(1, n_mem, m_heads, m_hd)
    k_sample = kf_s.reshape(db, 1, A_HEADS, A_HEAD_DIM)
    v_sample = vf_s.reshape(db, 1, A_HEADS, A_HEAD_DIM)
    conv_sample = jnp.concatenate([state_conv[:, 1:], hg_s[:, None, :]], axis=1)
    return (y_prompt, y_sample, k_prompt, v_prompt, conv_prompt, mem_k_prompt, mem_v_prompt,
            k_sample, v_sample, conv_sample)
```

```python
import functools
import math

import numpy as np
import jax
import jax.numpy as jnp
from jax import lax
from jax.experimental import pallas as pl
from jax.experimental.pallas import tpu as pltpu

F32 = jnp.float32
BF16 = jnp.bfloat16
I32 = jnp.int32

A_HEADS = 8
A_HEAD_DIM = 64
A_WIDTH = A_HEADS * A_HEAD_DIM
A_SCALE = A_HEAD_DIM ** -0.5
MOBA_BLOCK = 256
MOBA_TOPK = 3
ROPE_THETA = 10000.0
N_BRANCH = 3
TOP_K = 4
SWIGLU_LIMIT = 7.0
SWIGLU_ALPHA = 1.702
EPS = 1e-5
NEG = -1e30
LOG2E = math.log2(math.e)

LANES = 128
VMEM_LIMIT = 56 * 1024 * 1024
MOE_ROW_TILE = 256
V_AUG = A_HEAD_DIM + 16


def _cparams(sem, gather_dmas=False):
    return pltpu.CompilerParams(dimension_semantics=sem, vmem_limit_bytes=VMEM_LIMIT,
                                disable_bounds_checks=gather_dmas)


def _row_tile(n, cap):
    t = min(cap, n) // 8 * 8
    while n % t:
        t -= 8
    return t


def _const_spec(shape):
    nd = len(shape)
    return pl.BlockSpec(shape, lambda *_: (0,) * nd, pipeline_mode=pl.Buffered(1))


def _rms(x, g):
    return x * lax.rsqrt(jnp.mean(x * x, axis=-1, keepdims=True) + EPS) * g


def _split_bf16(x):
    hi = x.astype(BF16)
    lo = (x - hi.astype(F32)).astype(BF16)
    return hi, lo


def _dot(a, b):
    return jnp.dot(a, b, preferred_element_type=F32)


def _dot_t(a, b):
    return lax.dot_general(a, b, (((1,), (1,)), ((), ())), preferred_element_type=F32)


def _memkv_kernel(mem_ref, g_ref, w_ref, kv_ref):
    xn = _rms(mem_ref[...], g_ref[...])
    kv_ref[...] = _dot(xn.astype(BF16), w_ref[...].astype(BF16))


def _memory_kv(mem2d, g, w):
    n, d = mem2d.shape
    return pl.pallas_call(
        _memkv_kernel,
        out_shape=jax.ShapeDtypeStruct((n, w.shape[1]), F32),
        compiler_params=_cparams(None),
        name="memory_kv",
    )(mem2d, g.reshape(1, d), w)


def _inproj_kernel(x_ref, g_ref, w_ref, cos_ref, sin_ref, q_ref, kf_ref, vf_ref, h_ref, qm_ref, gt_ref,
                   *prompt_refs, m_scale):
    tm = x_ref.shape[0]
    aw = q_ref.shape[1]
    cw = h_ref.shape[1]
    mw = qm_ref.shape[1]
    xb = _rms(x_ref[...], g_ref[...]).astype(BF16)

    def proj(c0, n):
        return _dot(xb, w_ref[:, c0:c0 + n])

    cos = cos_ref[...]
    sin = sin_ref[...]
    lane = lax.broadcasted_iota(I32, (tm, LANES), 1)
    first_half = (lane % A_HEAD_DIM) < (A_HEAD_DIM // 2)

    def rope(c):
        swapped = jnp.where(first_half, pltpu.roll(c, LANES - A_HEAD_DIM // 2, 1),
                            pltpu.roll(c, A_HEAD_DIM // 2, 1))
        return c * cos + swapped * sin

    q = proj(0, aw)
    k = proj(aw, aw)
    for j in range(aw // LANES):
        sl = slice(j * LANES, (j + 1) * LANES)
        q_ref[:, sl] = rope(q[:, sl]).astype(BF16)
        kf_ref[:, sl] = rope(k[:, sl])
    v = proj(2 * aw, aw)
    vf_ref[...] = v
    if prompt_refs:
        ka_ref, vt_ref, ks_ref = prompt_refs
        row_blk = (pl.program_id(0) * tm + lax.broadcasted_iota(I32, (tm, LANES), 0)) // MOBA_BLOCK
        for j in range(aw // LANES):
            kc = kf_ref[:, j * LANES:(j + 1) * LANES] * (A_SCALE * LOG2E)
            low = lane < A_HEAD_DIM
            even = jnp.where(low, kc, jnp.where(lane - A_HEAD_DIM == row_blk, 1.0, 0.0))
            odd = jnp.where(low, jnp.where(lane == row_blk, 1.0, 0.0), kc)
            ka_ref[:, (2 * j) * LANES:(2 * j + 1) * LANES] = even.astype(BF16)
            ka_ref[:, (2 * j + 1) * LANES:(2 * j + 2) * LANES] = odd.astype(BF16)
        for b in range(tm // MOBA_BLOCK):
            ks_ref[b] = jnp.sum(kf_ref[b * MOBA_BLOCK:(b + 1) * MOBA_BLOCK, :], axis=0, keepdims=True)
        vt = v.T
        ones = jnp.ones((V_AUG - A_HEAD_DIM, tm), BF16)
        for h in range(A_HEADS):
            vt_ref[h * V_AUG:h * V_AUG + A_HEAD_DIM, :] = vt[h * A_HEAD_DIM:(h + 1) * A_HEAD_DIM, :].astype(BF16)
            vt_ref[h * V_AUG + A_HEAD_DIM:(h + 1) * V_AUG, :] = ones
    a = proj(3 * aw, cw)
    gate = proj(3 * aw + cw, cw)
    h_ref[...] = a * jax.nn.sigmoid(gate)
    c0 = 3 * aw + 2 * cw
    qm_ref[...] = (proj(c0, mw) * m_scale).astype(BF16)
    c0 += mw
    gw = gt_ref.shape[1]
    step = 512
    for j in range(gw // step):
        gt_ref[:, j * step:(j + 1) * step] = jax.nn.sigmoid(proj(c0 + j * step, step))


def _in_projection(x2d, g, w_bf, cos, sin, *, tm, cw, mw, m_scale, prompt):
    r, d = x2d.shape
    aw = A_WIDTH
    gw = w_bf.shape[1] - (3 * aw + 2 * cw + mw)
    row = lambda width: pl.BlockSpec((tm, width), lambda i: (i, 0))
    out_shape = [jax.ShapeDtypeStruct((r, aw), BF16), jax.ShapeDtypeStruct((r, aw), F32),
                 jax.ShapeDtypeStruct((r, aw), F32), jax.ShapeDtypeStruct((r, cw), F32),
                 jax.ShapeDtypeStruct((r, mw), BF16), jax.ShapeDtypeStruct((r, gw), F32)]
    out_specs = [row(aw), row(aw), row(aw), row(cw), row(mw), row(gw)]
    if prompt:
        assert tm % MOBA_BLOCK == 0
        bpt = tm // MOBA_BLOCK
        out_shape += [jax.ShapeDtypeStruct((r, A_HEADS * LANES), BF16),
                      jax.ShapeDtypeStruct((A_HEADS * V_AUG, r), BF16),
                      jax.ShapeDtypeStruct((r // MOBA_BLOCK, 1, aw), F32)]
        out_specs += [row(A_HEADS * LANES), pl.BlockSpec((A_HEADS * V_AUG, tm), lambda i: (0, i)),
                      pl.BlockSpec((bpt, 1, aw), lambda i: (i, 0, 0))]
    return pl.pallas_call(
        functools.partial(_inproj_kernel, m_scale=m_scale),
        grid=(r // tm,),
        in_specs=[row(d), _const_spec((1, d)), _const_spec(w_bf.shape), row(LANES), row(LANES)],
        out_specs=out_specs,
        out_shape=out_shape,
        compiler_params=_cparams(("arbitrary",)),
        name="in_projection",
    )(x2d, g.reshape(1, d), w_bf, cos, sin)


def _rope_tables(pos):
    half = A_HEAD_DIM // 2
    inv = ROPE_THETA ** (-jnp.arange(half, dtype=F32) / half)
    ang = pos.astype(F32)[:, None] * inv[None, :]
    cos = jnp.cos(ang)
    sin = jnp.sin(ang)
    reps = LANES // A_HEAD_DIM
    return (jnp.tile(jnp.concatenate([cos, cos], axis=-1), (1, reps)),
            jnp.tile(jnp.concatenate([-sin, sin], axis=-1), (1, reps)))


def _moba_kernel(qi_ref, ki_ref, q_ref, k_ref, vt_ref, ks_ref, o_ref, qaug, m_scr, acc, st_scr, *, qt):
    step = pl.program_id(0)
    qi = qi_ref[step]
    kb = ki_ref[step]
    dh = A_HEAD_DIM
    blk = MOBA_BLOCK
    n_chunks = A_WIDTH // LANES

    @pl.when(kb == 0)
    def _select():
        m_scr[...] = jnp.full(m_scr.shape, -jnp.inf, F32)
        acc[...] = jnp.zeros(acc.shape, F32)
        lane_k = lax.broadcasted_iota(I32, (dh, LANES), 1)
        rowid = lax.broadcasted_iota(I32, (dh, blk), 0)
        for s in range(qt):
            j = qi * qt + s
            qT = q_ref[s * blk:(s + 1) * blk, :].astype(F32).T
            for c in range(n_chunks):
                qTc = qT[c * LANES:(c + 1) * LANES, :].astype(BF16)
                ksc = ks_ref[:, c * LANES:(c + 1) * LANES]
                for par in range(2):
                    h = 2 * c + par
                    mine = (lane_k < dh) if par == 0 else (lane_k >= dh)
                    sc = _dot(jnp.where(mine, ksc, 0.0).astype(BF16), qTc)
                    sc = jnp.where(rowid < j, sc, NEG)
                    sel = rowid == j
                    for t in range(MOBA_TOPK):
                        mx = jnp.max(sc, axis=0, keepdims=True)
                        idx = jnp.min(jnp.where(sc == mx, rowid, dh), axis=0, keepdims=True)
                        pick = rowid == idx
                        sel = sel | (pick & (t < j))
                        sc = jnp.where(pick, -jnp.inf, sc)
                    bias = jnp.where(sel, 0.0, NEG).astype(BF16)
                    qh = qTc[par * dh:(par + 1) * dh, :]
                    qaug[s * A_HEADS + h, par * dh:(par + 1) * dh, :] = qh
                    qaug[s * A_HEADS + h, (1 - par) * dh:(2 - par) * dh, :] = bias

    def sweep(subs, masked):
        if masked:
            causal = lax.broadcasted_iota(I32, (blk, blk), 0) <= lax.broadcasted_iota(I32, (blk, blk), 1)
        chains = [(s, h) for s in subs for h in range(A_HEADS)]
        slots = st_scr.shape[0]

        def logits(i):
            s, h = chains[i]
            st_scr[i % slots] = _dot(k_ref[:, h * LANES:(h + 1) * LANES], qaug[s * A_HEADS + h])

        for i in range(min(slots - 1, len(chains))):
            logits(i)
        for i, (s, h) in enumerate(chains):
            if i + slots - 1 < len(chains):
                logits(i + slots - 1)
            a = s * A_HEADS + h
            st = st_scr[i % slots]
            if masked:
                st = jnp.where(causal, st, NEG)
            m_prev = m_scr[s, h:h + 1, :]
            m_new = jnp.maximum(m_prev, jnp.max(st, axis=0, keepdims=True))
            p = jnp.exp2(st - m_new).astype(BF16)
            acc[a] = jnp.exp2(m_prev - m_new) * acc[a] + _dot(vt_ref[h * V_AUG:(h + 1) * V_AUG, :], p)
            m_scr[s, h:h + 1, :] = m_new

    pl.when(kb < qi * qt)(functools.partial(sweep, list(range(qt)), False))
    for s in range(qt):
        j = qi * qt + s
        pl.when((kb >= qi * qt) & (kb < j))(functools.partial(sweep, [s], False))
        pl.when(kb == j)(functools.partial(sweep, [s], True))

    @pl.when(kb == qi * qt + qt - 1)
    def _finish():
        for s in range(qt):
            outs = []
            for h in range(A_HEADS):
                a = acc[s * A_HEADS + h]
                outs.append(a[0:dh, :] / a[dh:dh + 1, :])
            o_ref[s * blk:(s + 1) * blk, :] = jnp.concatenate(outs, axis=0).T.astype(BF16)


def _moba_prompt(q, ka, vt, ksum, *, qt):
    r, aw = q.shape
    nb = r // MOBA_BLOCK
    assert r % (qt * MOBA_BLOCK) == 0 and MOBA_TOPK <= nb <= A_HEAD_DIM and A_HEAD_DIM * 2 == LANES
    tq = qt * MOBA_BLOCK
    ks = jnp.zeros((A_HEAD_DIM, aw), F32).at[:nb].set(ksum)
    qi, ki = [], []
    for i in range(r // tq):
        for n in range((i + 1) * qt):
            qi.append(i)
            ki.append(n)
    grid_spec = pltpu.PrefetchScalarGridSpec(
        num_scalar_prefetch=2,
        grid=(len(qi),),
        in_specs=[pl.BlockSpec((tq, aw), lambda s, qi, ki: (qi[s], 0)),
                  pl.BlockSpec((MOBA_BLOCK, A_HEADS * LANES), lambda s, qi, ki: (ki[s], 0)),
                  pl.BlockSpec((A_HEADS * V_AUG, MOBA_BLOCK), lambda s, qi, ki: (0, ki[s])),
                  pl.BlockSpec(ks.shape, lambda s, qi, ki: (0, 0))],
        out_specs=pl.BlockSpec((tq, aw), lambda s, qi, ki: (qi[s], 0)),
        scratch_shapes=[pltpu.VMEM((qt * A_HEADS, LANES, MOBA_BLOCK), BF16),
                        pltpu.VMEM((qt, A_HEADS, MOBA_BLOCK), F32),
                        pltpu.VMEM((qt * A_HEADS, V_AUG, MOBA_BLOCK), F32),
                        pltpu.VMEM((9, MOBA_BLOCK, MOBA_BLOCK), F32)])
    return pl.pallas_call(
        functools.partial(_moba_kernel, qt=qt),
        grid_spec=grid_spec,
        out_shape=jax.ShapeDtypeStruct((r, aw), BF16),
        compiler_params=_cparams(("arbitrary",)),
        name="moba_prompt",
    )(jnp.asarray(np.array(qi, np.int32)), jnp.asarray(np.array(ki, np.int32)), q, ka, vt, ks)


def _ln_silu(y, g, b):
    mu = jnp.mean(y, axis=-1, keepdims=True)
    d = y - mu
    var = jnp.mean(d * d, axis=-1, keepdims=True)
    z = d * lax.rsqrt(var + EPS) * g + b
    return z * jax.nn.sigmoid(z)


def _conv_prompt_kernel(h_ref, st_ref, w_ref, b_ref, g_ref, beta_ref, o_ref, ext, *, halo, rc):
    tm = h_ref.shape[0]
    kw = w_ref.shape[0]

    @pl.when(pl.program_id(0) == 0)
    def _():
        ext[0:halo, :] = st_ref[...]

    ext[halo:halo + tm, :] = h_ref[...]
    off = halo - (kw - 1)
    for r0 in range(0, tm, rc):
        y = jnp.zeros((rc, h_ref.shape[1]), F32) + b_ref[...]
        for k in range(kw):
            y = y + w_ref[k:k + 1, :] * ext[r0 + off + k:r0 + off + k + rc, :]
        o_ref[r0:r0 + rc, :] = _ln_silu(y, g_ref[...], beta_ref[...]).astype(BF16)
    ext[0:halo, :] = ext[tm:tm + halo, :]


def _conv_prompt(h2d, state, w_dw, b_dw, ln_g, ln_b, *, tm):
    r, c = h2d.shape
    kw = w_dw.shape[0]
    halo = -(-(kw - 1) // 8) * 8
    st = jnp.zeros((halo, c), F32).at[halo - (kw - 1):].set(state)
    vec = lambda a: a.reshape(1, c)
    return pl.pallas_call(
        functools.partial(_conv_prompt_kernel, halo=halo, rc=64),
        grid=(r // tm,),
        in_specs=[pl.BlockSpec((tm, c), lambda i: (i, 0)), _const_spec((halo, c)), _const_spec((kw, c)),
                  _const_spec((1, c)), _const_spec((1, c)), _const_spec((1, c))],
        out_specs=pl.BlockSpec((tm, c), lambda i: (i, 0)),
        out_shape=jax.ShapeDtypeStruct((r, c), BF16),
        scratch_shapes=[pltpu.VMEM((tm + halo, c), F32)],
        compiler_params=_cparams(("arbitrary",)),
        name="conv_prompt",
    )(h2d, st, w_dw, vec(b_dw), vec(ln_g), vec(ln_b))


def _conv_step_kernel(st_ref, h_ref, w_ref, b_ref, g_ref, beta_ref, o_ref):
    kw = w_ref.shape[0]
    y = b_ref[...] + w_ref[kw - 1:kw, :] * h_ref[...]
    for k in range(kw - 1):
        y = y + w_ref[k:k + 1, :] * st_ref[k]
    o_ref[...] = _ln_silu(y, g_ref[...], beta_ref[...]).astype(BF16)


def _conv_step(state_t, h2d, w_dw, b_dw, ln_g, ln_b):
    b, c = h2d.shape
    vec = lambda a: a.reshape(1, c)
    return pl.pallas_call(
        _conv_step_kernel,
        out_shape=jax.ShapeDtypeStruct((b, c), BF16),
        compiler_params=_cparams(None),
        name="conv_step",
    )(state_t, h2d, w_dw, vec(b_dw), vec(ln_g), vec(ln_b))


def _memattn_kernel(q_ref, k_ref, v_ref, o_ref, *, hd):
    for h in range(q_ref.shape[1] // hd):
        sl = slice(h * hd, (h + 1) * hd)
        s = _dot_t(q_ref[:, sl], k_ref[:, sl])
        p = jnp.exp(s - jnp.max(s, axis=-1, keepdims=True))
        inv = 1.0 / jnp.sum(p, axis=-1, keepdims=True)
        o_ref[:, sl] = (_dot(p.astype(BF16), v_ref[:, sl]) * inv).astype(BF16)


def _memattn_prompt(qm, mk_bf, mv_bf, *, tm, hd):
    r, w = qm.shape
    return pl.pallas_call(
        functools.partial(_memattn_kernel, hd=hd),
        grid=(r // tm,),
        in_specs=[pl.BlockSpec((tm, w), lambda i: (i, 0)), _const_spec(mk_bf.shape), _const_spec(mv_bf.shape)],
        out_specs=pl.BlockSpec((tm, w), lambda i: (i, 0)),
        out_shape=jax.ShapeDtypeStruct((r, w), BF16),
        compiler_params=_cparams(("arbitrary",)),
        name="memattn_prompt",
    )(qm, mk_bf, mv_bf)


def _memattn_step_kernel(q_ref, k_ref, v_ref, o_ref):
    nh, hd = k_ref.shape[2:]
    for h in range(nh):
        lanes = slice(h * hd, (h + 1) * hd)
        q = q_ref[0, :, lanes].astype(F32)
        s = jnp.sum(k_ref[0, :, h, :] * q, axis=-1, keepdims=True)
        p = jnp.exp(s - jnp.max(s, axis=0, keepdims=True))
        num = jnp.sum(p * v_ref[0, :, h, :], axis=0, keepdims=True)
        o_ref[0, :, lanes] = (num / jnp.sum(p, axis=0, keepdims=True)).astype(BF16)


def _memattn_step(q, k, v):
    b, w = q.shape
    nk, nh, hd = k.shape[1:]
    one = pl.BlockSpec((1, 1, w), lambda i: (i, 0, 0))
    many = pl.BlockSpec((1, nk, nh, hd), lambda i: (i, 0, 0, 0))
    out = pl.pallas_call(
        _memattn_step_kernel,
        grid=(b,),
        in_specs=[one, many, many],
        out_specs=one,
        out_shape=jax.ShapeDtypeStruct((b, 1, w), BF16),
        compiler_params=_cparams(("arbitrary",)),
        name="memattn_step",
    )(q.reshape(b, 1, w), k, v)
    return out.reshape(b, w)


def _block_select_kernel(pt_ref, q_ref, *refs, pages_per_block, n_blocks):
    del pt_ref
    page_refs = refs[:-3]
    top_ref, qb, sc = refs[-3:]
    s = pl.program_id(1)
    nh, dh, page = page_refs[0].shape[1:]
    bps = len(page_refs) // pages_per_block
    sub = 8

    @pl.when(s == 0)
    def _():
        qb[...] = jnp.broadcast_to(q_ref[0], qb.shape)

    for m in range(bps):
        part = None
        for j in range(pages_per_block):
            x = page_refs[m * pages_per_block + j][0].reshape(nh * dh, page) * qb[...]
            x = jnp.sum(x.reshape(nh, dh // sub, sub, page), axis=1)
            part = x if part is None else part + x
        tot = jnp.sum(jnp.sum(part, axis=2, keepdims=True), axis=1)
        sc[s * bps + m] = tot * (1.0 / MOBA_BLOCK)

    @pl.when(s == pl.num_programs(1) - 1)
    def _():
        scores = sc[...]
        ids = lax.broadcasted_iota(I32, scores.shape, 0)
        for t in range(MOBA_TOPK):
            mx = jnp.max(scores, axis=0, keepdims=True)
            idx = jnp.min(jnp.where(scores == mx, ids, n_blocks), axis=0, keepdims=True)
            top_ref[0, t] = idx[0]
            scores = jnp.where(ids == idx, -jnp.inf, scores)


def _block_select(cache_kt, page_table, q_col, *, pages_per_step):
    db, n_pages = page_table.shape
    _, nh, dh, page = cache_kt.shape
    ppb = MOBA_BLOCK // page
    n_blocks = n_pages // ppb
    assert n_pages % pages_per_step == 0 and pages_per_step % ppb == 0 and n_blocks >= MOBA_TOPK

    def page_spec(j):
        return pl.BlockSpec((1, nh, dh, page), lambda b, s, pt: (pt[b, s * pages_per_step + j], 0, 0, 0))

    grid_spec = pltpu.PrefetchScalarGridSpec(
        num_scalar_prefetch=1,
        grid=(db, n_pages // pages_per_step),
        in_specs=[pl.BlockSpec((1, nh * dh, 1), lambda b, s, pt: (b, 0, 0))]
        + [page_spec(j) for j in range(pages_per_step)],
        out_specs=pl.BlockSpec((1, MOBA_TOPK, nh, 1), lambda b, s, pt: (b, 0, 0, 0)),
        scratch_shapes=[pltpu.VMEM((nh * dh, page), F32), pltpu.VMEM((n_blocks, nh, 1), F32)])
    return pl.pallas_call(
        functools.partial(_block_select_kernel, pages_per_block=ppb, n_blocks=n_blocks),
        grid_spec=grid_spec,
        out_shape=jax.ShapeDtypeStruct((db, MOBA_TOPK, nh, 1), I32),
        compiler_params=_cparams(("arbitrary", "arbitrary")),
        name="block_select",
    )(page_table, q_col, *([cache_kt] * pages_per_step))


def _moba_step_kernel(pg_ref, q_ref, kn_ref, vn_ref, kt_hbm, vt_hbm, o_ref, kbuf, vbuf, sem, *, n_tiles):
    b = pl.program_id(0)
    dh = A_HEAD_DIM

    def tile_copies(seq, slot):
        out = []
        for h in range(A_HEADS):
            for t in range(n_tiles):
                pg = pg_ref[(seq * A_HEADS + h) * n_tiles + t]
                out.append(pltpu.make_async_copy(kt_hbm.at[pg, h], kbuf.at[slot, h, t], sem.at[slot, 0]))
                out.append(pltpu.make_async_copy(vt_hbm.at[pg, h], vbuf.at[slot, h, t], sem.at[slot, 1]))
        return out

    @pl.when(b == 0)
    def _():
        for cp in tile_copies(0, 0):
            cp.start()

    @pl.when(b + 1 < pl.num_programs(0))
    def _():
        for cp in tile_copies(b + 1, (b + 1) % 2):
            cp.start()

    slot = b % 2
    for cp in tile_copies(b, slot):
        cp.wait()

    for h in range(A_HEADS):
        rows = slice(h * dh, (h + 1) * dh)
        q = q_ref[0, rows, :]
        s_new = jnp.sum(q * kn_ref[0, rows, :], axis=0, keepdims=True) * A_SCALE
        ss = [jnp.sum(kbuf[slot, h, t] * q, axis=0, keepdims=True) * A_SCALE
              for t in range(n_tiles)]
        m = s_new
        for s in ss:
            m = jnp.maximum(m, jnp.max(s, axis=-1, keepdims=True))
        p_new = jnp.exp(s_new - m)
        den = p_new
        num = p_new * vn_ref[0, rows, :]
        acc = None
        for t in range(n_tiles):
            p = jnp.exp(ss[t] - m)
            den = den + jnp.sum(p, axis=-1, keepdims=True)
            pv = vbuf[slot, h, t] * p
            acc = pv if acc is None else acc + pv
        num = num + jnp.sum(acc, axis=-1, keepdims=True)
        o_ref[0, rows, :] = num / den


def _moba_step(cache_kt, cache_vt, pages, q_col, kn_col, vn_col):
    db, w, _ = q_col.shape
    _, nh, dh, page = cache_kt.shape
    n_tiles = pages.shape[0] // (db * nh)
    col = pl.BlockSpec((1, w, 1), lambda b, pg: (b, 0, 0))
    hbm = pl.BlockSpec(memory_space=pl.ANY)
    grid_spec = pltpu.PrefetchScalarGridSpec(
        num_scalar_prefetch=1,
        grid=(db,),
        in_specs=[col, col, col, hbm, hbm],
        out_specs=col,
        scratch_shapes=[pltpu.VMEM((2, nh, n_tiles, dh, page), F32), pltpu.VMEM((2, nh, n_tiles, dh, page), F32),
                        pltpu.SemaphoreType.DMA((2, 2))])
    return pl.pallas_call(
        functools.partial(_moba_step_kernel, n_tiles=n_tiles),
        grid_spec=grid_spec,
        out_shape=jax.ShapeDtypeStruct((db, w, 1), F32),
        compiler_params=_cparams(("arbitrary",), gather_dmas=True),
        name="moba_step",
    )(pages, q_col, kn_col, vn_col, cache_kt, cache_vt)


def _merge_kernel(x_ref, oa_ref, oc_ref, om_ref, gt_ref, wa_ref, wc_ref, wm_ref, wo_ref, g_ref, wr_ref, br_ref,
                  cin_ref, h_ref, hn_ref, te_ref, tw_ref, rk_ref, cnt_ref, carry):
    d = x_ref.shape[1]
    tm = x_ref.shape[0]
    merged = (gt_ref[:, 0:d] * _dot(oa_ref[...], wa_ref[...])
              + gt_ref[:, d:2 * d] * _dot(oc_ref[...], wc_ref[...])
              + gt_ref[:, 2 * d:3 * d] * _dot(om_ref[...], wm_ref[...]))
    h = x_ref[...] + _dot(merged.astype(BF16), wo_ref[...])
    h_ref[...] = h
    hn = _rms(h, g_ref[...])
    hn_ref[...] = hn
    hi, lo = _split_bf16(hn)
    whi, wlo = _split_bf16(wr_ref[...])
    logits = _dot(hi, whi) + _dot(lo, whi) + _dot(hi, wlo) + br_ref[...]
    ne = logits.shape[1]
    lane = lax.broadcasted_iota(I32, (tm, ne), 1)
    out_lane = lax.broadcasted_iota(I32, (tm, LANES), 1)
    s = logits
    vals, picks, te = [], [], jnp.zeros((tm, LANES), I32)
    for t in range(TOP_K):
        mx = jnp.max(s, axis=-1, keepdims=True)
        idx = jnp.min(jnp.where(s == mx, lane, ne), axis=-1, keepdims=True)
        vals.append(mx)
        picks.append(lane == idx)
        te = jnp.where(out_lane == t, idx, te)
        s = jnp.where(picks[-1], -jnp.inf, s)
    es = [jnp.exp(v - vals[0]) for v in vals]
    inv = 1.0 / functools.reduce(lambda a, b: a + b, es)
    tw = jnp.zeros((tm, LANES), F32)
    for t in range(TOP_K):
        tw = jnp.where(out_lane == t, es[t] * inv, tw)
    te_ref[...] = te
    tw_ref[...] = tw

    @pl.when(pl.program_id(0) == 0)
    def _():
        carry[...] = cin_ref[...]

    chosen = functools.reduce(lambda a, b: a | b, picks)
    earlier = lax.broadcasted_iota(I32, (tm, tm), 1) < lax.broadcasted_iota(I32, (tm, tm), 0)
    before = carry[...] + _dot(jnp.where(earlier, 1.0, 0.0).astype(BF16), jnp.where(chosen, 1.0, 0.0).astype(BF16))
    rk = jnp.zeros((tm, LANES), I32)
    for t in range(TOP_K):
        r_t = jnp.sum(jnp.where(picks[t], before, 0.0), axis=-1, keepdims=True)
        rk = jnp.where(out_lane == t, r_t.astype(I32), rk)
    rk_ref[...] = rk
    carry[...] = carry[...] + jnp.sum(jnp.where(chosen, 1.0, 0.0), axis=0, keepdims=True)
    cnt_ref[...] = carry[...]


def _merge(x2d, oa, oc, om, gates, wa, wc, wm, wo, g, wr, br, counts_in, *, tm):
    r, d = x2d.shape
    ne = wr.shape[1]
    row = lambda width: pl.BlockSpec((tm, width), lambda i: (i, 0))
    return pl.pallas_call(
        _merge_kernel,
        grid=(r // tm,),
        in_specs=[row(d), row(oa.shape[1]), row(oc.shape[1]), row(om.shape[1]), row(gates.shape[1]),
                  _const_spec(wa.shape), _const_spec(wc.shape), _const_spec(wm.shape), _const_spec(wo.shape),
                  _const_spec((1, d)), _const_spec(wr.shape), _const_spec((1, ne)), _const_spec((1, ne))],
        out_specs=[row(d), row(d), row(LANES), row(LANES), row(LANES), pl.BlockSpec((1, ne), lambda i: (0, 0))],
        out_shape=[jax.ShapeDtypeStruct((r, d), F32), jax.ShapeDtypeStruct((r, d), F32),
                   jax.ShapeDtypeStruct((r, LANES), I32), jax.ShapeDtypeStruct((r, LANES), F32),
                   jax.ShapeDtypeStruct((r, LANES), I32), jax.ShapeDtypeStruct((1, ne), F32)],
        scratch_shapes=[pltpu.VMEM((1, ne), F32)],
        compiler_params=_cparams(("arbitrary",)),
        name="merge",
    )(x2d, oa, oc, om, gates, wa, wc, wm, wo, g.reshape(1, d), wr, br.reshape(1, ne), counts_in)


def _moe_kernel(be_ref, nu_ref, c0_ref, tok_ref, x_hbm, wgu_ref, bgu_ref, wd_ref, bd_ref, o_ref, wgu_b, wd_b,
                xbuf, sem):
    i = pl.program_id(0)
    nu = nu_ref[0]
    te = xbuf.shape[1]
    e = be_ref[i]
    prev = be_ref[jnp.maximum(i - 1, 0)]

    def row_copies(tile, slot):
        c0 = c0_ref[tile]
        return [pltpu.make_async_copy(x_hbm.at[pl.ds(tok_ref[c0 + j], 1)], xbuf.at[slot, pl.ds(j, 1)],
                                      sem.at[slot]) for j in range(te)]

    @pl.when(i == 0)
    def _():
        for cp in row_copies(0, 0):
            cp.start()

    @pl.when(i <= nu)
    def _():
        for cp in row_copies(i, i % 2):
            cp.wait()

    @pl.when((i == 0) | (e != prev))
    def _():
        wgu_b[...] = wgu_ref[0].astype(BF16)
        wd_b[...] = wd_ref[0].astype(BF16)

    @pl.when(i < nu)
    def _():
        for cp in row_copies(i + 1, (i + 1) % 2):
            cp.start()
        f = wd_b.shape[0]
        gu = _dot(xbuf[i % 2].astype(BF16), wgu_b[...]) + bgu_ref[0]
        g = jnp.minimum(gu[:, :f], SWIGLU_LIMIT)
        u = jnp.clip(gu[:, f:], -SWIGLU_LIMIT, SWIGLU_LIMIT)
        hdn = g * jax.nn.sigmoid(SWIGLU_ALPHA * g) * (u + 1.0)
        o_ref[...] = _dot(hdn.astype(BF16), wd_b[...]) + bd_ref[0]

    @pl.when(i >= nu)
    def _():
        o_ref[...] = jnp.zeros(o_ref.shape, F32)


def _moe_experts(x, plan, w_gate_up, b_gate_up, w_down, b_down):
    blk_e, n_used, tile_c0, sorted_tok = plan
    d = x.shape[1]
    ne, _, f2 = w_gate_up.shape
    f = f2 // 2
    te = MOE_ROW_TILE
    n_tiles = blk_e.shape[0]
    grid_spec = pltpu.PrefetchScalarGridSpec(
        num_scalar_prefetch=4,
        grid=(n_tiles,),
        in_specs=[pl.BlockSpec(memory_space=pl.ANY),
                  pl.BlockSpec((1, d, f2), lambda i, be, *_: (be[i], 0, 0)),
                  pl.BlockSpec((1, 1, f2), lambda i, be, *_: (be[i], 0, 0)),
                  pl.BlockSpec((1, f, d), lambda i, be, *_: (be[i], 0, 0)),
                  pl.BlockSpec((1, 1, d), lambda i, be, *_: (be[i], 0, 0))],
        out_specs=pl.BlockSpec((te, d), lambda i, *_: (i, 0)),
        scratch_shapes=[pltpu.VMEM((d, f2), BF16), pltpu.VMEM((f, d), BF16),
                        pltpu.VMEM((2, te, d), F32), pltpu.SemaphoreType.DMA((2,))])
    return pl.pallas_call(
        _moe_kernel,
        grid_spec=grid_spec,
        out_shape=jax.ShapeDtypeStruct((n_tiles * te, d), F32),
        compiler_params=_cparams(("arbitrary",), gather_dmas=True),
        name="moe_experts",
    )(blk_e, n_used, tile_c0, sorted_tok, x, w_gate_up, b_gate_up.reshape(ne, 1, f2), w_down,
      b_down.reshape(ne, 1, d))


def _moe_plan(top_e, rank, counts, n_tok):
    ne_tile = MOE_ROW_TILE
    ne = counts.shape[0]
    n_asg = n_tok * TOP_K
    start = jnp.cumsum(counts) - counts
    padded = (counts + ne_tile - 1) // ne_tile * ne_tile
    pad_end = jnp.cumsum(padded)
    pad_start = pad_end - padded
    onehot = top_e[:, :, None] == jnp.arange(ne, dtype=I32)[None, None, :]
    pick = lambda table: jnp.sum(jnp.where(onehot, table[None, None, :], 0), axis=-1)
    dest = pick(pad_start) + rank
    tok_ids = jnp.repeat(jnp.arange(n_tok, dtype=I32), TOP_K)
    _, sorted_tok = lax.sort_key_val((pick(start) + rank).reshape(-1).astype(I32), tok_ids)
    sorted_tok = jnp.concatenate([sorted_tok, jnp.zeros((ne_tile,), I32)])
    n_tiles = n_asg // ne_tile + ne + 1
    tile_row = jnp.arange(n_tiles, dtype=I32) * ne_tile
    blk_e = jnp.minimum(jnp.sum((pad_end[None, :] <= tile_row[:, None]).astype(I32), axis=1), ne - 1)
    tile_c0 = jnp.clip(start[blk_e] + tile_row - pad_start[blk_e], 0, n_asg - 1)
    n_used = (pad_end[-1] // ne_tile).astype(I32).reshape(1)
    return dest.astype(I32), (blk_e.astype(I32), n_used, tile_c0.astype(I32), sorted_tok)


def _combine_kernel(h_ref, rows_ref, tw_ref, g_ref, y_ref):
    tw = tw_ref[...]
    h = h_ref[...]
    for t in range(rows_ref.shape[0]):
        h = h + tw[:, t:t + 1] * rows_ref[t]
    y_ref[...] = _rms(h, g_ref[...])


def _combine(h, rows, tw, g, *, tm):
    r, d = h.shape
    k = rows.shape[0]
    return pl.pallas_call(
        _combine_kernel,
        grid=(r // tm,),
        in_specs=[pl.BlockSpec((tm, d), lambda i: (i, 0)), pl.BlockSpec((k, tm, d), lambda i: (0, i, 0)),
                  pl.BlockSpec((tm, LANES), lambda i: (i, 0)), _const_spec((1, d))],
        out_specs=pl.BlockSpec((tm, d), lambda i: (i, 0)),
        out_shape=jax.ShapeDtypeStruct((r, d), F32),
        compiler_params=_cparams(("arbitrary",)),
        name="combine",
    )(h, rows, tw, g.reshape(1, d))


def kernel(x_prompt, x_sample, mem_prompt, cache_k, cache_v, page_table, state_conv, cache_mem_k, cache_mem_v,
           norm_mix_g, w_in, w_dw, b_dw, conv_ln_g, conv_ln_b, norm_mem_g, w_mem_kv, w_branch_a, w_branch_c,
           w_branch_m, w_out, norm_ffn_g, w_router, b_router, w_gate_up, b_gate_up, w_down, b_down, norm_final_g):
    bp, seq, d = x_prompt.shape
    db, dseq, _ = x_sample.shape
    assert bp == 1 and dseq == 1
    n_mem, m_heads, m_hd = cache_mem_k.shape[1:]
    mw = m_heads * m_hd
    m_scale = m_hd ** -0.5
    cw = w_dw.shape[1]
    kw = w_dw.shape[0]
    page = cache_k.shape[1]
    n_pages = page_table.shape[1]
    past = n_pages * page
    assert past % MOBA_BLOCK == 0 and cache_k.shape[2:] == (A_HEADS, A_HEAD_DIM)

    w_in_bf = w_in.astype(BF16)
    wa, wc, wm, wo = (w.astype(BF16) for w in (w_branch_a, w_branch_c, w_branch_m, w_out))

    xp = x_prompt.reshape(seq, d)
    kv_mem = _memory_kv(mem_prompt.reshape(n_mem, d), norm_mem_g, w_mem_kv)
    mk_p, mv_p = kv_mem[:, :mw], kv_mem[:, mw:]
    cos_p, sin_p = _rope_tables(jnp.arange(seq, dtype=I32))
    q_p, kf_p, vf_p, hg_p, qm_p, gt_p, kb_p, vt_p, ksum_p = _in_projection(
        xp, norm_mix_g, w_in_bf, cos_p, sin_p, tm=512, cw=cw, mw=mw, m_scale=m_scale, prompt=True)
    oa_p = _moba_prompt(q_p, kb_p, vt_p, ksum_p.reshape(seq // MOBA_BLOCK, A_WIDTH), qt=4)
    oc_p = _conv_prompt(hg_p, jnp.zeros((kw - 1, cw), F32), w_dw, b_dw, conv_ln_g, conv_ln_b,
                        tm=_row_tile(seq, 512))
    om_p = _memattn_prompt(qm_p, mk_p.astype(BF16), mv_p.astype(BF16), tm=_row_tile(seq, 512), hd=m_hd)
    n_exp = w_router.shape[1]
    h_p, hn_p, te_p, tw_p, rk_p, cnt_p = _merge(xp, oa_p, oc_p, om_p, gt_p, wa, wc, wm, wo, norm_ffn_g, w_router,
                                                b_router, jnp.zeros((1, n_exp), F32), tm=_row_tile(seq, 512))

    xs = x_sample.reshape(db, d)
    cos_s, sin_s = _rope_tables(jnp.full((db,), past, I32))
    q_s, kf_s, vf_s, hg_s, qm_s, gt_s = _in_projection(
        xs, norm_mix_g, w_in_bf, cos_s, sin_s, tm=db, cw=cw, mw=mw, m_scale=m_scale, prompt=False)
    cache_kt = cache_k.transpose(0, 2, 3, 1)
    cache_vt = cache_v.transpose(0, 2, 3, 1)
    col = lambda a: a.astype(F32).reshape(db, A_WIDTH, 1)
    q_col = col(q_s)
    top = _block_select(cache_kt, page_table, q_col, pages_per_step=min(32, n_pages))
    ppb = MOBA_BLOCK // page
    pages = page_table[jnp.arange(db)[:, None, None, None],
                       top[:, :, :, 0].transpose(0, 2, 1)[..., None] * ppb + jnp.arange(ppb)]
    oa_s = _moba_step(cache_kt, cache_vt, pages.reshape(-1), q_col, col(kf_s), col(vf_s))
    oa_s = oa_s.reshape(db, A_WIDTH).astype(BF16)
    oc_s = _conv_step(state_conv.transpose(1, 0, 2), hg_s, w_dw, b_dw, conv_ln_g, conv_ln_b)
    om_s = _memattn_step(qm_s, cache_mem_k, cache_mem_v)
    h_s, hn_s, te_s, tw_s, rk_s, cnt = _merge(xs, oa_s, oc_s, om_s, gt_s, wa, wc, wm, wo, norm_ffn_g, w_router,
                                              b_router, cnt_p, tm=db)

    n_tok = seq + db
    hn = jnp.concatenate([hn_p, hn_s], axis=0)
    top_e = jnp.concatenate([te_p[:, :TOP_K], te_s[:, :TOP_K]], axis=0)
    rank = jnp.concatenate([rk_p[:, :TOP_K], rk_s[:, :TOP_K]], axis=0)
    tw = jnp.concatenate([tw_p, tw_s], axis=0)
    h_all = jnp.concatenate([h_p, h_s], axis=0)
    dest, plan = _moe_plan(top_e, rank, cnt[0].astype(I32), n_tok)
    out_rows = _moe_experts(hn, plan, w_gate_up, b_gate_up, w_down, b_down)
    y = _combine(h_all, out_rows[dest.T], tw, norm_final_g, tm=_row_tile(n_tok, 384))

    y_prompt = y[:seq].reshape(1, seq, d)
    y_sample = y[seq:].reshape(db, 1, d)
    k_prompt = kf_p.reshape(1, seq, A_HEADS, A_HEAD_DIM)
    v_prompt = vf_p.reshape(1, seq, A_HEADS, A_HEAD_DIM)
    conv_prompt = hg_p[seq - (kw - 1):].reshape(1, kw - 1, cw)
    mem_k_prompt = mk_p.reshape(1, n_mem, m_heads, m_hd)
    mem_v_prompt = mv_p.reshape(1, n_mem, m_heads, m_hd)
    k_sample = kf_s.reshape(db, 1, A_HEADS, A_HEAD_DIM)
    v_sample = vf_s.reshape(db, 1, A_HEADS, A_HEAD_DIM)
    conv_sample = jnp.concatenate([state_conv[:, 1:], hg_s[:, None, :]], axis=1)
    return (y_prompt, y_sample, k_prompt, v_prompt, conv_prompt, mem_k_prompt, mem_v_prompt,
            k_sample, v_sample, conv_sample)
```

```python
import functools
import math

import numpy as np
import jax
import jax.numpy as jnp
from jax import lax
from jax.experimental import pallas as pl
from jax.experimental.pallas import tpu as pltpu

F32 = jnp.float32
BF16 = jnp.bfloat16
I32 = jnp.int32

A_HEADS = 8
A_HEAD_DIM = 64
A_WIDTH = A_HEADS * A_HEAD_DIM
A_SCALE = A_HEAD_DIM ** -0.5
MOBA_BLOCK = 256
MOBA_TOPK = 3
ROPE_THETA = 10000.0
N_BRANCH = 3
TOP_K = 4
SWIGLU_LIMIT = 7.0
SWIGLU_ALPHA = 1.702
EPS = 1e-5
NEG = -1e30
LOG2E = math.log2(math.e)

LANES = 128
VMEM_LIMIT = 56 * 1024 * 1024
MOE_ROW_TILE = 256
V_AUG = A_HEAD_DIM + 16


def _cparams(sem, gather_dmas=False):
    return pltpu.CompilerParams(dimension_semantics=sem, vmem_limit_bytes=VMEM_LIMIT,
                                disable_bounds_checks=gather_dmas)


def _row_tile(n, cap):
    t = min(cap, n) // 8 * 8
    while n % t:
        t -= 8
    return t


def _const_spec(shape):
    nd = len(shape)
    return pl.BlockSpec(shape, lambda *_: (0,) * nd, pipeline_mode=pl.Buffered(1))


def _rms(x, g):
    return x * lax.rsqrt(jnp.mean(x * x, axis=-1, keepdims=True) + EPS) * g


def _split_bf16(x):
    hi = x.astype(BF16)
    lo = (x - hi.astype(F32)).astype(BF16)
    return hi, lo


def _dot(a, b):
    return jnp.dot(a, b, preferred_element_type=F32)


def _dot_t(a, b):
    return lax.dot_general(a, b, (((1,), (1,)), ((), ())), preferred_element_type=F32)


def _memkv_kernel(mem_ref, g_ref, w_ref, kv_ref):
    xn = _rms(mem_ref[...], g_ref[...])
    kv_ref[...] = _dot(xn.astype(BF16), w_ref[...].astype(BF16))


def _memory_kv(mem2d, g, w):
    n, d = mem2d.shape
    return pl.pallas_call(
        _memkv_kernel,
        out_shape=jax.ShapeDtypeStruct((n, w.shape[1]), F32),
        compiler_params=_cparams(None),
        name="memory_kv",
    )(mem2d, g.reshape(1, d), w)


def _inproj_kernel(x_ref, g_ref, w_ref, cos_ref, sin_ref, q_ref, kf_ref, vf_ref, h_ref, qm_ref, gt_ref,
                   *prompt_refs, m_scale):
    tm = x_ref.shape[0]
    aw = q_ref.shape[1]
    cw = h_ref.shape[1]
    mw = qm_ref.shape[1]
    xb = _rms(x_ref[...], g_ref[...]).astype(BF16)

    def proj(c0, n):
        return _dot(xb, w_ref[:, c0:c0 + n])

    cos = cos_ref[...]
    sin = sin_ref[...]
    lane = lax.broadcasted_iota(I32, (tm, LANES), 1)
    first_half = (lane % A_HEAD_DIM) < (A_HEAD_DIM // 2)

    def rope(c):
        swapped = jnp.where(first_half, pltpu.roll(c, LANES - A_HEAD_DIM // 2, 1),
                            pltpu.roll(c, A_HEAD_DIM // 2, 1))
        return c * cos + swapped * sin

    q = proj(0, aw)
    k = proj(aw, aw)
    for j in range(aw // LANES):
        sl = slice(j * LANES, (j + 1) * LANES)
        q_ref[:, sl] = rope(q[:, sl]).astype(BF16)
        kf_ref[:, sl] = rope(k[:, sl])
    v = proj(2 * aw, aw)
    vf_ref[...] = v
    if prompt_refs:
        ka_ref, vt_ref, ks_ref = prompt_refs
        row_blk = (pl.program_id(0) * tm + lax.broadcasted_iota(I32, (tm, LANES), 0)) // MOBA_BLOCK
        for j in range(aw // LANES):
            kc = kf_ref[:, j * LANES:(j + 1) * LANES] * (A_SCALE * LOG2E)
            low = lane < A_HEAD_DIM
            even = jnp.where(low, kc, jnp.where(lane - A_HEAD_DIM == row_blk, 1.0, 0.0))
            odd = jnp.where(low, jnp.where(lane == row_blk, 1.0, 0.0), kc)
            ka_ref[:, (2 * j) * LANES:(2 * j + 1) * LANES] = even.astype(BF16)
            ka_ref[:, (2 * j + 1) * LANES:(2 * j + 2) * LANES] = odd.astype(BF16)
        for b in range(tm // MOBA_BLOCK):
            ks_ref[b] = jnp.sum(kf_ref[b * MOBA_BLOCK:(b + 1) * MOBA_BLOCK, :], axis=0, keepdims=True)
        vt = v.T
        ones = jnp.ones((V_AUG - A_HEAD_DIM, tm), BF16)
        for h in range(A_HEADS):
            vt_ref[h * V_AUG:h * V_AUG + A_HEAD_DIM, :] = vt[h * A_HEAD_DIM:(h + 1) * A_HEAD_DIM, :].astype(BF16)
            vt_ref[h * V_AUG + A_HEAD_DIM:(h + 1) * V_AUG, :] = ones
    a = proj(3 * aw, cw)
    gate = proj(3 * aw + cw, cw)
    h_ref[...] = a * jax.nn.sigmoid(gate)
    c0 = 3 * aw + 2 * cw
    qm_ref[...] = (proj(c0, mw) * m_scale).astype(BF16)
    c0 += mw
    gw = gt_ref.shape[1]
    step = 512
    for j in range(gw // step):
        gt_ref[:, j * step:(j + 1) * step] = jax.nn.sigmoid(proj(c0 + j * step, step))


def _in_projection(x2d, g, w_bf, cos, sin, *, tm, cw, mw, m_scale, prompt):
    r, d = x2d.shape
    aw = A_WIDTH
    gw = w_bf.shape[1] - (3 * aw + 2 * cw + mw)
    row = lambda width: pl.BlockSpec((tm, width), lambda i: (i, 0))
    out_shape = [jax.ShapeDtypeStruct((r, aw), BF16), jax.ShapeDtypeStruct((r, aw), F32),
                 jax.ShapeDtypeStruct((r, aw), F32), jax.ShapeDtypeStruct((r, cw), F32),
                 jax.ShapeDtypeStruct((r, mw), BF16), jax.ShapeDtypeStruct((r, gw), F32)]
    out_specs = [row(aw), row(aw), row(aw), row(cw), row(mw), row(gw)]
    if prompt:
        assert tm % MOBA_BLOCK == 0
        bpt = tm // MOBA_BLOCK
        out_shape += [jax.ShapeDtypeStruct((r, A_HEADS * LANES), BF16),
                      jax.ShapeDtypeStruct((A_HEADS * V_AUG, r), BF16),
                      jax.ShapeDtypeStruct((r // MOBA_BLOCK, 1, aw), F32)]
        out_specs += [row(A_HEADS * LANES), pl.BlockSpec((A_HEADS * V_AUG, tm), lambda i: (0, i)),
                      pl.BlockSpec((bpt, 1, aw), lambda i: (i, 0, 0))]
    return pl.pallas_call(
        functools.partial(_inproj_kernel, m_scale=m_scale),
        grid=(r // tm,),
        in_specs=[row(d), _const_spec((1, d)), _const_spec(w_bf.shape), row(LANES), row(LANES)],
        out_specs=out_specs,
        out_shape=out_shape,
        compiler_params=_cparams(("arbitrary",)),
        name="in_projection",
    )(x2d, g.reshape(1, d), w_bf, cos, sin)


def _rope_tables(pos):
    half = A_HEAD_DIM // 2
    inv = ROPE_THETA ** (-jnp.arange(half, dtype=F32) / half)
    ang = pos.astype(F32)[:, None] * inv[None, :]
    cos = jnp.cos(ang)
    sin = jnp.sin(ang)
    reps = LANES // A_HEAD_DIM
    return (jnp.tile(jnp.concatenate([cos, cos], axis=-1), (1, reps)),
            jnp.tile(jnp.concatenate([-sin, sin], axis=-1), (1, reps)))


def _moba_kernel(qi_ref, ki_ref, q_ref, k_ref, vt_ref, ks_ref, o_ref, qaug, m_scr, acc, st_scr, *, qt):
    step = pl.program_id(0)
    qi = qi_ref[step]
    kb = ki_ref[step]
    dh = A_HEAD_DIM
    blk = MOBA_BLOCK
    n_chunks = A_WIDTH // LANES

    @pl.when(kb == 0)
    def _select():
        m_scr[...] = jnp.full(m_scr.shape, -jnp.inf, F32)
        acc[...] = jnp.zeros(acc.shape, F32)
        lane_k = lax.broadcasted_iota(I32, (dh, LANES), 1)
        rowid = lax.broadcasted_iota(I32, (dh, blk), 0)
        for s in range(qt):
            j = qi * qt + s
            qT = q_ref[s * blk:(s + 1) * blk, :].astype(F32).T
            for c in range(n_chunks):
                qTc = qT[c * LANES:(c + 1) * LANES, :].astype(BF16)
                ksc = ks_ref[:, c * LANES:(c + 1) * LANES]
                for par in range(2):
                    h = 2 * c + par
                    mine = (lane_k < dh) if par == 0 else (lane_k >= dh)
                    sc = _dot(jnp.where(mine, ksc, 0.0).astype(BF16), qTc)
                    sc = jnp.where(rowid < j, sc, NEG)
                    sel = rowid == j
                    for t in range(MOBA_TOPK):
                        mx = jnp.max(sc, axis=0, keepdims=True)
                        idx = jnp.min(jnp.where(sc == mx, rowid, dh), axis=0, keepdims=True)
                        pick = rowid == idx
                        sel = sel | (pick & (t < j))
                        sc = jnp.where(pick, -jnp.inf, sc)
                    bias = jnp.where(sel, 0.0, NEG).astype(BF16)
                    qh = qTc[par * dh:(par + 1) * dh, :]
                    qaug[s * A_HEADS + h, par * dh:(par + 1) * dh, :] = qh
                    qaug[s * A_HEADS + h, (1 - par) * dh:(2 - par) * dh, :] = bias

    def sweep(subs, masked):
        if masked:
            causal = lax.broadcasted_iota(I32, (blk, blk), 0) <= lax.broadcasted_iota(I32, (blk, blk), 1)
        chains = [(s, h) for s in subs for h in range(A_HEADS)]
        slots = st_scr.shape[0]

        def logits(i):
            s, h = chains[i]
            st_scr[i % slots] = _dot(k_ref[:, h * LANES:(h + 1) * LANES], qaug[s * A_HEADS + h])

        for i in range(min(slots - 1, len(chains))):
            logits(i)
        for i, (s, h) in enumerate(chains):
            if i + slots - 1 < len(chains):
                logits(i + slots - 1)
            a = s * A_HEADS + h
            st = st_scr[i % slots]
            if masked:
                st = jnp.where(causal, st, NEG)
            m_prev = m_scr[s, h:h + 1, :]
            m_new = jnp.maximum(m_prev, jnp.max(st, axis=0, keepdims=True))
            p = jnp.exp2(st - m_new).astype(BF16)
            acc[a] = jnp.exp2(m_prev - m_new) * acc[a] + _dot(vt_ref[h * V_AUG:(h + 1) * V_AUG, :], p)
            m_scr[s, h:h + 1, :] = m_new

    pl.when(kb < qi * qt)(functools.partial(sweep, list(range(qt)), False))
    for s in range(qt):
        j = qi * qt + s
        pl.when((kb >= qi * qt) & (kb < j))(functools.partial(sweep, [s], False))
        pl.when(kb == j)(functools.partial(sweep, [s], True))

    @pl.when(kb == qi * qt + qt - 1)
    def _finish():
        for s in range(qt):
            outs = []
            for h in range(A_HEADS):
                a = acc[s * A_HEADS + h]
                outs.append(a[0:dh, :] / a[dh:dh + 1, :])
            o_ref[s * blk:(s + 1) * blk, :] = jnp.concatenate(outs, axis=0).T.astype(BF16)


def _moba_prompt(q, ka, vt, ksum, *, qt):
    r, aw = q.shape
    nb = r // MOBA_BLOCK
    assert r % (qt * MOBA_BLOCK) == 0 and MOBA_TOPK <= nb <= A_HEAD_DIM and A_HEAD_DIM * 2 == LANES
    tq = qt * MOBA_BLOCK
    ks = jnp.zeros((A_HEAD_DIM, aw), F32).at[:nb].set(ksum)
    qi, ki = [], []
    for i in range(r // tq):
        for n in range((i + 1) * qt):
            qi.append(i)
            ki.append(n)
    grid_spec = pltpu.PrefetchScalarGridSpec(
        num_scalar_prefetch=2,
        grid=(len(qi),),
        in_specs=[pl.BlockSpec((tq, aw), lambda s, qi, ki: (qi[s], 0)),
                  pl.BlockSpec((MOBA_BLOCK, A_HEADS * LANES), lambda s, qi, ki: (ki[s], 0)),
                  pl.BlockSpec((A_HEADS * V_AUG, MOBA_BLOCK), lambda s, qi, ki: (0, ki[s])),
                  pl.BlockSpec(ks.shape, lambda s, qi, ki: (0, 0))],
        out_specs=pl.BlockSpec((tq, aw), lambda s, qi, ki: (qi[s], 0)),
        scratch_shapes=[pltpu.VMEM((qt * A_HEADS, LANES, MOBA_BLOCK), BF16),
                        pltpu.VMEM((qt, A_HEADS, MOBA_BLOCK), F32),
                        pltpu.VMEM((qt * A_HEADS, V_AUG, MOBA_BLOCK), F32),
                        pltpu.VMEM((9, MOBA_BLOCK, MOBA_BLOCK), F32)])
    return pl.pallas_call(
        functools.partial(_moba_kernel, qt=qt),
        grid_spec=grid_spec,
        out_shape=jax.ShapeDtypeStruct((r, aw), BF16),
        compiler_params=_cparams(("arbitrary",)),
        name="moba_prompt",
    )(jnp.asarray(np.array(qi, np.int32)), jnp.asarray(np.array(ki, np.int32)), q, ka, vt, ks)


def _ln_silu(y, g, b):
    mu = jnp.mean(y, axis=-1, keepdims=True)
    d = y - mu
    var = jnp.mean(d * d, axis=-1, keepdims=True)
    z = d * lax.rsqrt(var + EPS) * g + b
    return z * jax.nn.sigmoid(z)


def _conv_prompt_kernel(h_ref, st_ref, w_ref, b_ref, g_ref, beta_ref, o_ref, ext, *, halo, rc):
    tm = h_ref.shape[0]
    kw = w_ref.shape[0]

    @pl.when(pl.program_id(0) == 0)
    def _():
        ext[0:halo, :] = st_ref[...]

    ext[halo:halo + tm, :] = h_ref[...]
    off = halo - (kw - 1)
    for r0 in range(0, tm, rc):
        y = jnp.zeros((rc, h_ref.shape[1]), F32) + b_ref[...]
        for k in range(kw):
            y = y + w_ref[k:k + 1, :] * ext[r0 + off + k:r0 + off + k + rc, :]
        o_ref[r0:r0 + rc, :] = _ln_silu(y, g_ref[...], beta_ref[...]).astype(BF16)
    ext[0:halo, :] = ext[tm:tm + halo, :]


def _conv_prompt(h2d, state, w_dw, b_dw, ln_g, ln_b, *, tm):
    r, c = h2d.shape
    kw = w_dw.shape[0]
    halo = -(-(kw - 1) // 8) * 8
    st = jnp.zeros((halo, c), F32).at[halo - (kw - 1):].set(state)
    vec = lambda a: a.reshape(1, c)
    return pl.pallas_call(
        functools.partial(_conv_prompt_kernel, halo=halo, rc=64),
        grid=(r // tm,),
        in_specs=[pl.BlockSpec((tm, c), lambda i: (i, 0)), _const_spec((halo, c)), _const_spec((kw, c)),
                  _const_spec((1, c)), _const_spec((1, c)), _const_spec((1, c))],
        out_specs=pl.BlockSpec((tm, c), lambda i: (i, 0)),
        out_shape=jax.ShapeDtypeStruct((r, c), BF16),
        scratch_shapes=[pltpu.VMEM((tm + halo, c), F32)],
        compiler_params=_cparams(("arbitrary",)),
        name="conv_prompt",
    )(h2d, st, w_dw, vec(b_dw), vec(ln_g), vec(ln_b))


def _conv_step_kernel(st_ref, h_ref, w_ref, b_ref, g_ref, beta_ref, o_ref):
    kw = w_ref.shape[0]
    y = b_ref[...] + w_ref[kw - 1:kw, :] * h_ref[...]
    for k in range(kw - 1):
        y = y + w_ref[k:k + 1, :] * st_ref[k]
    o_ref[...] = _ln_silu(y, g_ref[...], beta_ref[...]).astype(BF16)


def _conv_step(state_t, h2d, w_dw, b_dw, ln_g, ln_b):
    b, c = h2d.shape
    vec = lambda a: a.reshape(1, c)
    return pl.pallas_call(
        _conv_step_kernel,
        out_shape=jax.ShapeDtypeStruct((b, c), BF16),
        compiler_params=_cparams(None),
        name="conv_step",
    )(state_t, h2d, w_dw, vec(b_dw), vec(ln_g), vec(ln_b))


def _memattn_kernel(q_ref, k_ref, v_ref, o_ref, *, hd):
    for h in range(q_ref.shape[1] // hd):
        sl = slice(h * hd, (h + 1) * hd)
        s = _dot_t(q_ref[:, sl], k_ref[:, sl])
        p = jnp.exp(s - jnp.max(s, axis=-1, keepdims=True))
        inv = 1.0 / jnp.sum(p, axis=-1, keepdims=True)
        o_ref[:, sl] = (_dot(p.astype(BF16), v_ref[:, sl]) * inv).astype(BF16)


def _memattn_prompt(qm, mk_bf, mv_bf, *, tm, hd):
    r, w = qm.shape
    return pl.pallas_call(
        functools.partial(_memattn_kernel, hd=hd),
        grid=(r // tm,),
        in_specs=[pl.BlockSpec((tm, w), lambda i: (i, 0)), _const_spec(mk_bf.shape), _const_spec(mv_bf.shape)],
        out_specs=pl.BlockSpec((tm, w), lambda i: (i, 0)),
        out_shape=jax.ShapeDtypeStruct((r, w), BF16),
        compiler_params=_cparams(("arbitrary",)),
        name="memattn_prompt",
    )(qm, mk_bf, mv_bf)


def _memattn_step_kernel(q_ref, k_ref, v_ref, o_ref):
    nh, hd = k_ref.shape[2:]
    for h in range(nh):
        lanes = slice(h * hd, (h + 1) * hd)
        q = q_ref[0, :, lanes].astype(F32)
        s = jnp.sum(k_ref[0, :, h, :] * q, axis=-1, keepdims=True)
        p = jnp.exp(s - jnp.max(s, axis=0, keepdims=True))
        num = jnp.sum(p * v_ref[0, :, h, :], axis=0, keepdims=True)
        o_ref[0, :, lanes] = (num / jnp.sum(p, axis=0, keepdims=True)).astype(BF16)


def _memattn_step(q, k, v):
    b, w = q.shape
    nk, nh, hd = k.shape[1:]
    one = pl.BlockSpec((1, 1, w), lambda i: (i, 0, 0))
    many = pl.BlockSpec((1, nk, nh, hd), lambda i: (i, 0, 0, 0))
    out = pl.pallas_call(
        _memattn_step_kernel,
        grid=(b,),
        in_specs=[one, many, many],
        out_specs=one,
        out_shape=jax.ShapeDtypeStruct((b, 1, w), BF16),
        compiler_params=_cparams(("arbitrary",)),
        name="memattn_step",
    )(q.reshape(b, 1, w), k, v)
    return out.reshape(b, w)


def _block_select_kernel(pt_ref, q_ref, *refs, pages_per_block, n_blocks):
    del pt_ref
    page_refs = refs[:-3]
    top_ref, qb, sc = refs[-3:]
    s = pl.program_id(1)
    nh, dh, page = page_refs[0].shape[1:]
    bps = len(page_refs) // pages_per_block
    sub = 8

    @pl.when(s == 0)
    def _():
        qb[...] = jnp.broadcast_to(q_ref[0], qb.shape)

    for m in range(bps):
        part = None
        for j in range(pages_per_block):
            x = page_refs[m * pages_per_block + j][0].reshape(nh * dh, page) * qb[...]
            x = jnp.sum(x.reshape(nh, dh // sub, sub, page), axis=1)
            part = x if part is None else part + x
        tot = jnp.sum(jnp.sum(part, axis=2, keepdims=True), axis=1)
        sc[s * bps + m] = tot * (1.0 / MOBA_BLOCK)

    @pl.when(s == pl.num_programs(1) - 1)
    def _():
        scores = sc[...]
        ids = lax.broadcasted_iota(I32, scores.shape, 0)
        for t in range(MOBA_TOPK):
            mx = jnp.max(scores, axis=0, keepdims=True)
            idx = jnp.min(jnp.where(scores == mx, ids, n_blocks), axis=0, keepdims=True)
            top_ref[0, t] = idx[0]
            scores = jnp.where(ids == idx, -jnp.inf, scores)


def _block_select(cache_kt, page_table, q_col, *, pages_per_step):
    db, n_pages = page_table.shape
    _, nh, dh, page = cache_kt.shape
    ppb = MOBA_BLOCK // page
    n_blocks = n_pages // ppb
    assert n_pages % pages_per_step == 0 and pages_per_step % ppb == 0 and n_blocks >= MOBA_TOPK

    def page_spec(j):
        return pl.BlockSpec((1, nh, dh, page), lambda b, s, pt: (pt[b, s * pages_per_step + j], 0, 0, 0))

    grid_spec = pltpu.PrefetchScalarGridSpec(
        num_scalar_prefetch=1,
        grid=(db, n_pages // pages_per_step),
        in_specs=[pl.BlockSpec((1, nh * dh, 1), lambda b, s, pt: (b, 0, 0))]
        + [page_spec(j) for j in range(pages_per_step)],
        out_specs=pl.BlockSpec((1, MOBA_TOPK, nh, 1), lambda b, s, pt: (b, 0, 0, 0)),
        scratch_shapes=[pltpu.VMEM((nh * dh, page), F32), pltpu.VMEM((n_blocks, nh, 1), F32)])
    return pl.pallas_call(
        functools.partial(_block_select_kernel, pages_per_block=ppb, n_blocks=n_blocks),
        grid_spec=grid_spec,
        out_shape=jax.ShapeDtypeStruct((db, MOBA_TOPK, nh, 1), I32),
        compiler_params=_cparams(("arbitrary", "arbitrary")),
        name="block_select",
    )(page_table, q_col, *([cache_kt] * pages_per_step))


def _moba_step_kernel(pg_ref, q_ref, kn_ref, vn_ref, kt_hbm, vt_hbm, o_ref, kbuf, vbuf, sem, *, n_tiles):
    b = pl.program_id(0)
    dh = A_HEAD_DIM

    def tile_copies(seq, slot):
        out = []
        for h in range(A_HEADS):
            for t in range(n_tiles):
                pg = pg_ref[(seq * A_HEADS + h) * n_tiles + t]
                out.append(pltpu.make_async_copy(kt_hbm.at[pg, h], kbuf.at[slot, h, t], sem.at[slot, 0]))
                out.append(pltpu.make_async_copy(vt_hbm.at[pg, h], vbuf.at[slot, h, t], sem.at[slot, 1]))
        return out

    @pl.when(b == 0)
    def _():
        for cp in tile_copies(0, 0):
            cp.start()

    @pl.when(b + 1 < pl.num_programs(0))
    def _():
        for cp in tile_copies(b + 1, (b + 1) % 2):
            cp.start()

    slot = b % 2
    for cp in tile_copies(b, slot):
        cp.wait()

    for h in range(A_HEADS):
        rows = slice(h * dh, (h + 1) * dh)
        q = q_ref[0, rows, :]
        s_new = jnp.sum(q * kn_ref[0, rows, :], axis=0, keepdims=True) * A_SCALE
        ss = [jnp.sum(kbuf[slot, h, t] * q, axis=0, keepdims=True) * A_SCALE
              for t in range(n_tiles)]
        m = s_new
        for s in ss:
            m = jnp.maximum(m, jnp.max(s, axis=-1, keepdims=True))
        p_new = jnp.exp(s_new - m)
        den = p_new
        num = p_new * vn_ref[0, rows, :]
        acc = None
        for t in range(n_tiles):
            p = jnp.exp(ss[t] - m)
            den = den + jnp.sum(p, axis=-1, keepdims=True)
            pv = vbuf[slot, h, t] * p
            acc = pv if acc is None else acc + pv
        num = num + jnp.sum(acc, axis=-1, keepdims=True)
        o_ref[0, rows, :] = num / den


def _moba_step(cache_kt, cache_vt, pages, q_col, kn_col, vn_col):
    db, w, _ = q_col.shape
    _, nh, dh, page = cache_kt.shape
    n_tiles = pages.shape[0] // (db * nh)
    col = pl.BlockSpec((1, w, 1), lambda b, pg: (b, 0, 0))
    hbm = pl.BlockSpec(memory_space=pl.ANY)
    grid_spec = pltpu.PrefetchScalarGridSpec(
        num_scalar_prefetch=1,
        grid=(db,),
        in_specs=[col, col, col, hbm, hbm],
        out_specs=col,
        scratch_shapes=[pltpu.VMEM((2, nh, n_tiles, dh, page), F32), pltpu.VMEM((2, nh, n_tiles, dh, page), F32),
                        pltpu.SemaphoreType.DMA((2, 2))])
    return pl.pallas_call(
        functools.partial(_moba_step_kernel, n_tiles=n_tiles),
        grid_spec=grid_spec,
        out_shape=jax.ShapeDtypeStruct((db, w, 1), F32),
        compiler_params=_cparams(("arbitrary",), gather_dmas=True),
        name="moba_step",
    )(pages, q_col, kn_col, vn_col, cache_kt, cache_vt)


def _merge_kernel(x_ref, oa_ref, oc_ref, om_ref, gt_ref, wa_ref, wc_ref, wm_ref, wo_ref, g_ref, wr_ref, br_ref,
                  cin_ref, h_ref, hn_ref, te_ref, tw_ref, rk_ref, cnt_ref, carry):
    d = x_ref.shape[1]
    tm = x_ref.shape[0]
    merged = (gt_ref[:, 0:d] * _dot(oa_ref[...], wa_ref[...])
              + gt_ref[:, d:2 * d] * _dot(oc_ref[...], wc_ref[...])
              + gt_ref[:, 2 * d:3 * d] * _dot(om_ref[...], wm_ref[...]))
    h = x_ref[...] + _dot(merged.astype(BF16), wo_ref[...])
    h_ref[...] = h
    hn = _rms(h, g_ref[...])
    hn_ref[...] = hn
    hi, lo = _split_bf16(hn)
    whi, wlo = _split_bf16(wr_ref[...])
    logits = _dot(hi, whi) + _dot(lo, whi) + _dot(hi, wlo) + br_ref[...]
    ne = logits.shape[1]
    lane = lax.broadcasted_iota(I32, (tm, ne), 1)
    out_lane = lax.broadcasted_iota(I32, (tm, LANES), 1)
    s = logits
    vals, picks, te = [], [], jnp.zeros((tm, LANES), I32)
    for t in range(TOP_K):
        mx = jnp.max(s, axis=-1, keepdims=True)
        idx = jnp.min(jnp.where(s == mx, lane, ne), axis=-1, keepdims=True)
        vals.append(mx)
        picks.append(lane == idx)
        te = jnp.where(out_lane == t, idx, te)
        s = jnp.where(picks[-1], -jnp.inf, s)
    es = [jnp.exp(v - vals[0]) for v in vals]
    inv = 1.0 / functools.reduce(lambda a, b: a + b, es)
    tw = jnp.zeros((tm, LANES), F32)
    for t in range(TOP_K):
        tw = jnp.where(out_lane == t, es[t] * inv, tw)
    te_ref[...] = te
    tw_ref[...] = tw

    @pl.when(pl.program_id(0) == 0)
    def _():
        carry[...] = cin_ref[...]

    chosen = functools.reduce(lambda a, b: a | b, picks)
    earlier = lax.broadcasted_iota(I32, (tm, tm), 1) < lax.broadcasted_iota(I32, (tm, tm), 0)
    before = carry[...] + _dot(jnp.where(earlier, 1.0, 0.0).astype(BF16), jnp.where(chosen, 1.0, 0.0).astype(BF16))
    rk = jnp.zeros((tm, LANES), I32)
    for t in range(TOP_K):
        r_t = jnp.sum(jnp.where(picks[t], before, 0.0), axis=-1, keepdims=True)
        rk = jnp.where(out_lane == t, r_t.astype(I32), rk)
    rk_ref[...] = rk
    carry[...] = carry[...] + jnp.sum(jnp.where(chosen, 1.0, 0.0), axis=0, keepdims=True)
    cnt_ref[...] = carry[...]


def _merge(x2d, oa, oc, om, gates, wa, wc, wm, wo, g, wr, br, counts_in, *, tm):
    r, d = x2d.shape
    ne = wr.shape[1]
    row = lambda width: pl.BlockSpec((tm, width), lambda i: (i, 0))
    return pl.pallas_call(
        _merge_kernel,
        grid=(r // tm,),
        in_specs=[row(d), row(oa.shape[1]), row(oc.shape[1]), row(om.shape[1]), row(gates.shape[1]),
                  _const_spec(wa.shape), _const_spec(wc.shape), _const_spec(wm.shape), _const_spec(wo.shape),
                  _const_spec((1, d)), _const_spec(wr.shape), _const_spec((1, ne)), _const_spec((1, ne))],
        out_specs=[row(d), row(d), row(LANES), row(LANES), row(LANES), pl.BlockSpec((1, ne), lambda i: (0, 0))],
        out_shape=[jax.ShapeDtypeStruct((r, d), F32), jax.ShapeDtypeStruct((r, d), F32),
                   jax.ShapeDtypeStruct((r, LANES), I32), jax.ShapeDtypeStruct((r, LANES), F32),
                   jax.ShapeDtypeStruct((r, LANES), I32), jax.ShapeDtypeStruct((1, ne), F32)],
        scratch_shapes=[pltpu.VMEM((1, ne), F32)],
        compiler_params=_cparams(("arbitrary",)),
        name="merge",
    )(x2d, oa, oc, om, gates, wa, wc, wm, wo, g.reshape(1, d), wr, br.reshape(1, ne), counts_in)


def _moe_kernel(be_ref, nu_ref, c0_ref, tok_ref, x_hbm, wgu_ref, bgu_ref, wd_ref, bd_ref, o_ref, wgu_b, wd_b,
                xbuf0, xbuf1, sem):
    i = pl.program_id(0)
    nu = nu_ref[0]
    bufs = (xbuf0, xbuf1)
    te = xbuf0.shape[0]
    e = be_ref[i]
    prev = be_ref[jnp.maximum(i - 1, 0)]

    def row_copies(tile, slot):
        c0 = c0_ref[tile]
        return [pltpu.make_async_copy(x_hbm.at[pl.ds(tok_ref[c0 + j], 1)], bufs[slot].at[pl.ds(j, 1)],
                                      sem.at[slot]) for j in range(te)]

    @pl.when(i == 0)
    def _():
        for cp in row_copies(0, 0):
            cp.start()

    for slot in range(2):
        @pl.when((i <= nu) & (i % 2 == slot))
        def _():
            for cp in row_copies(i, slot):
                cp.wait()

    @pl.when((i == 0) | (e != prev))
    def _():
        wgu_b[...] = wgu_ref[0].astype(BF16)
        wd_b[...] = wd_ref[0].astype(BF16)

    def compute(slot):
        for cp in row_copies(i + 1, 1 - slot):
            cp.start()
        f = wd_b.shape[0]
        gu = _dot(bufs[slot][...].astype(BF16), wgu_b[...]) + bgu_ref[0]
        g = jnp.minimum(gu[:, :f], SWIGLU_LIMIT)
        u = jnp.clip(gu[:, f:], -SWIGLU_LIMIT, SWIGLU_LIMIT)
        hdn = g * jax.nn.sigmoid(SWIGLU_ALPHA * g) * (u + 1.0)
        o_ref[...] = _dot(hdn.astype(BF16), wd_b[...]) + bd_ref[0]

    for slot in range(2):
        pl.when((i < nu) & (i % 2 == slot))(functools.partial(compute, slot))

    @pl.when(i >= nu)
    def _():
        o_ref[...] = jnp.zeros(o_ref.shape, F32)


def _moe_experts(x, plan, w_gate_up, b_gate_up, w_down, b_down):
    blk_e, n_used, tile_c0, sorted_tok = plan
    d = x.shape[1]
    ne, _, f2 = w_gate_up.shape
    f = f2 // 2
    te = MOE_ROW_TILE
    n_tiles = blk_e.shape[0]
    grid_spec = pltpu.PrefetchScalarGridSpec(
        num_scalar_prefetch=4,
        grid=(n_tiles,),
        in_specs=[pl.BlockSpec(memory_space=pl.ANY),
                  pl.BlockSpec((1, d, f2), lambda i, be, *_: (be[i], 0, 0)),
                  pl.BlockSpec((1, 1, f2), lambda i, be, *_: (be[i], 0, 0)),
                  pl.BlockSpec((1, f, d), lambda i, be, *_: (be[i], 0, 0)),
                  pl.BlockSpec((1, 1, d), lambda i, be, *_: (be[i], 0, 0))],
        out_specs=pl.BlockSpec((te, d), lambda i, *_: (i, 0)),
        scratch_shapes=[pltpu.VMEM((d, f2), BF16), pltpu.VMEM((f, d), BF16),
                        pltpu.VMEM((te, d), F32), pltpu.VMEM((te, d), F32), pltpu.SemaphoreType.DMA((2,))])
    return pl.pallas_call(
        _moe_kernel,
        grid_spec=grid_spec,
        out_shape=jax.ShapeDtypeStruct((n_tiles * te, d), F32),
        compiler_params=_cparams(("arbitrary",), gather_dmas=True),
        name="moe_experts",
    )(blk_e, n_used, tile_c0, sorted_tok, x, w_gate_up, b_gate_up.reshape(ne, 1, f2), w_down,
      b_down.reshape(ne, 1, d))


def _moe_plan(top_e, rank, counts, n_tok):
    ne_tile = MOE_ROW_TILE
    ne = counts.shape[0]
    n_asg = n_tok * TOP_K
    start = jnp.cumsum(counts) - counts
    padded = (counts + ne_tile - 1) // ne_tile * ne_tile
    pad_end = jnp.cumsum(padded)
    pad_start = pad_end - padded
    onehot = top_e[:, :, None] == jnp.arange(ne, dtype=I32)[None, None, :]
    pick = lambda table: jnp.sum(jnp.where(onehot, table[None, None, :], 0), axis=-1)
    dest = pick(pad_start) + rank
    tok_ids = jnp.repeat(jnp.arange(n_tok, dtype=I32), TOP_K)
    _, sorted_tok = lax.sort_key_val((pick(start) + rank).reshape(-1).astype(I32), tok_ids)
    sorted_tok = jnp.concatenate([sorted_tok, jnp.zeros((ne_tile,), I32)])
    n_tiles = n_asg // ne_tile + ne + 1
    tile_row = jnp.arange(n_tiles, dtype=I32) * ne_tile
    blk_e = jnp.minimum(jnp.sum((pad_end[None, :] <= tile_row[:, None]).astype(I32), axis=1), ne - 1)
    tile_c0 = jnp.clip(start[blk_e] + tile_row - pad_start[blk_e], 0, n_asg - 1)
    n_used = (pad_end[-1] // ne_tile).astype(I32).reshape(1)
    return dest.astype(I32), (blk_e.astype(I32), n_used, tile_c0.astype(I32), sorted_tok)


def _combine_kernel(h_ref, rows_ref, tw_ref, g_ref, y_ref):
    tw = tw_ref[...]
    h = h_ref[...]
    for t in range(rows_ref.shape[0]):
        h = h + tw[:, t:t + 1] * rows_ref[t]
    y_ref[...] = _rms(h, g_ref[...])


def _combine(h, rows, tw, g, *, tm):
    r, d = h.shape
    k = rows.shape[0]
    return pl.pallas_call(
        _combine_kernel,
        grid=(r // tm,),
        in_specs=[pl.BlockSpec((tm, d), lambda i: (i, 0)), pl.BlockSpec((k, tm, d), lambda i: (0, i, 0)),
                  pl.BlockSpec((tm, LANES), lambda i: (i, 0)), _const_spec((1, d))],
        out_specs=pl.BlockSpec((tm, d), lambda i: (i, 0)),
        out_shape=jax.ShapeDtypeStruct((r, d), F32),
        compiler_params=_cparams(("arbitrary",)),
        name="combine",
    )(h, rows, tw, g.reshape(1, d))


def kernel(x_prompt, x_sample, mem_prompt, cache_k, cache_v, page_table, state_conv, cache_mem_k, cache_mem_v,
           norm_mix_g, w_in, w_dw, b_dw, conv_ln_g, conv_ln_b, norm_mem_g, w_mem_kv, w_branch_a, w_branch_c,
           w_branch_m, w_out, norm_ffn_g, w_router, b_router, w_gate_up, b_gate_up, w_down, b_down, norm_final_g):
    bp, seq, d = x_prompt.shape
    db, dseq, _ = x_sample.shape
    assert bp == 1 and dseq == 1
    n_mem, m_heads, m_hd = cache_mem_k.shape[1:]
    mw = m_heads * m_hd
    m_scale = m_hd ** -0.5
    cw = w_dw.shape[1]
    kw = w_dw.shape[0]
    page = cache_k.shape[1]
    n_pages = page_table.shape[1]
    past = n_pages * page
    assert past % MOBA_BLOCK == 0 and cache_k.shape[2:] == (A_HEADS, A_HEAD_DIM)

    w_in_bf = w_in.astype(BF16)
    wa, wc, wm, wo = (w.astype(BF16) for w in (w_branch_a, w_branch_c, w_branch_m, w_out))

    xp = x_prompt.reshape(seq, d)
    kv_mem = _memory_kv(mem_prompt.reshape(n_mem, d), norm_mem_g, w_mem_kv)
    mk_p, mv_p = kv_mem[:, :mw], kv_mem[:, mw:]
    cos_p, sin_p = _rope_tables(jnp.arange(seq, dtype=I32))
    q_p, kf_p, vf_p, hg_p, qm_p, gt_p, kb_p, vt_p, ksum_p = _in_projection(
        xp, norm_mix_g, w_in_bf, cos_p, sin_p, tm=512, cw=cw, mw=mw, m_scale=m_scale, prompt=True)
    qt = next(t for t in (4, 2, 1) if (seq // MOBA_BLOCK) % t == 0)
    oa_p = _moba_prompt(q_p, kb_p, vt_p, ksum_p.reshape(seq // MOBA_BLOCK, A_WIDTH), qt=qt)
    oc_p = _conv_prompt(hg_p, jnp.zeros((kw - 1, cw), F32), w_dw, b_dw, conv_ln_g, conv_ln_b,
                        tm=_row_tile(seq, 512))
    om_p = _memattn_prompt(qm_p, mk_p.astype(BF16), mv_p.astype(BF16), tm=_row_tile(seq, 512), hd=m_hd)
    n_exp = w_router.shape[1]
    h_p, hn_p, te_p, tw_p, rk_p, cnt_p = _merge(xp, oa_p, oc_p, om_p, gt_p, wa, wc, wm, wo, norm_ffn_g, w_router,
                                                b_router, jnp.zeros((1, n_exp), F32), tm=_row_tile(seq, 512))

    xs = x_sample.reshape(db, d)
    cos_s, sin_s = _rope_tables(jnp.full((db,), past, I32))
    q_s, kf_s, vf_s, hg_s, qm_s, gt_s = _in_projection(
        xs, norm_mix_g, w_in_bf, cos_s, sin_s, tm=db, cw=cw, mw=mw, m_scale=m_scale, prompt=False)
    cache_kt = cache_k.transpose(0, 2, 3, 1)
    cache_vt = cache_v.transpose(0, 2, 3, 1)
    col = lambda a: a.astype(F32).reshape(db, A_WIDTH, 1)
    q_col = col(q_s)
    top = _block_select(cache_kt, page_table, q_col, pages_per_step=min(32, n_pages))
    ppb = MOBA_BLOCK // page
    pages = page_table[jnp.arange(db)[:, None, None, None],
                       top[:, :, :, 0].transpose(0, 2, 1)[..., None] * ppb + jnp.arange(ppb)]
    oa_s = _moba_step(cache_kt, cache_vt, pages.reshape(-1), q_col, col(kf_s), col(vf_s))
    oa_s = oa_s.reshape(db, A_WIDTH).astype(BF16)
    oc_s = _conv_step(state_conv.transpose(1, 0, 2), hg_s, w_dw, b_dw, conv_ln_g, conv_ln_b)
    om_s = _memattn_step(qm_s, cache_mem_k, cache_mem_v)
    h_s, hn_s, te_s, tw_s, rk_s, cnt = _merge(xs, oa_s, oc_s, om_s, gt_s, wa, wc, wm, wo, norm_ffn_g, w_router,
                                              b_router, cnt_p, tm=db)

    n_tok = seq + db
    hn = jnp.concatenate([hn_p, hn_s], axis=0)
    top_e = jnp.concatenate([te_p[:, :TOP_K], te_s[:, :TOP_K]], axis=0)
    rank = jnp.concatenate([rk_p[:, :TOP_K], rk_s[:, :TOP_K]], axis=0)
    tw = jnp.concatenate([tw_p, tw_s], axis=0)
    h_all = jnp.concatenate([h_p, h_s], axis=0)
    dest, plan = _moe_plan(top_e, rank, cnt[0].astype(I32), n_tok)
    out_rows = _moe_experts(hn, plan, w_gate_up, b_gate_up, w_down, b_down)
    y = _combine(h_all, out_rows[dest.T], tw, norm_final_g, tm=_row_tile(n_tok, 384))

    y_prompt = y[:seq].reshape(1, seq, d)
    y_sample = y[seq:].reshape(db, 1, d)
    k_prompt = kf_p.reshape(1, seq, A_HEADS, A_HEAD_DIM)
    v_prompt = vf_p.reshape(1, seq, A_HEADS, A_HEAD_DIM)
    conv_prompt = hg_p[seq - (kw - 1):].reshape(1, kw - 1, cw)
    mem_k_prompt = mk_p.reshape(1, n_mem, m_heads, m_hd)
    mem_v_prompt = mv_p.reshape(1, n_mem, m_heads, m_hd)
    k_sample = kf_s.reshape(db, 1, A_HEADS, A_HEAD_DIM)
    v_sample = vf_s.reshape(db, 1, A_HEADS, A_HEAD_DIM)
    conv_sample = jnp.concatenate([state_conv[:, 1:], hg_s[:, None, :]], axis=1)
    return (y_prompt, y_sample, k_prompt, v_prompt, conv_prompt, mem_k_prompt, mem_v_prompt,
            k_sample, v_sample, conv_sample)
```
